```python
import math, functools
import jax, jax.numpy as jnp
from jax import lax
import numpy as np

D_MODEL = 1024
BATCH = 16
SEQ = 2048
DEPTH = 2
DEC_BATCH = 128
DEC_SEQ = 4
PAST_LEN = 16384
PAGE_SIZE = 128

N_META = 16
D_MIX = D_MODEL
N_HEADS = 8
QK_NOPE = 64
QK_ROPE = 32
V_DIM = 64
Q_LORA = 384
KV_LORA = 256
ROPE_THETA = 10000.0
D_RNN = D_MIX - N_HEADS * V_DIM
N_RNN_BLOCKS = 8
RNN_BLOCK = D_RNN // N_RNN_BLOCKS
CONV_W = 4
LRU_C = 8.0
D_FF = -(-8 * D_MODEL // (3 * 256)) * 256
Q_BLOCK = 128
RMS_EPS = 1e-6
IN_W = Q_LORA + KV_LORA + QK_ROPE + 2 * D_RNN
SPLIT_AT = (Q_LORA, Q_LORA + KV_LORA, Q_LORA + KV_LORA + QK_ROPE, Q_LORA + KV_LORA + QK_ROPE + D_RNN)
ATTN_SCALE = 1.0 / math.sqrt(QK_NOPE + QK_ROPE)

kernel_name = "hymba_mla_rglru_decode_step"


def rmsnorm(x, g):
    xf = x.astype(jnp.float32)
    y = xf * lax.rsqrt(jnp.mean(xf * xf, axis=-1, keepdims=True) + RMS_EPS)
    return (y * g.astype(jnp.float32)).astype(x.dtype)


def rope(x, pos):
    half = QK_ROPE // 2
    inv = ROPE_THETA ** (-jnp.arange(half, dtype=jnp.float32) / half)
    ang = pos[:, None] * inv[None, :]
    cos = jnp.cos(ang)[None, :, None, :]
    sin = jnp.sin(ang)[None, :, None, :]
    xf = x.astype(jnp.float32)
    x1, x2 = xf[..., :half], xf[..., half:]
    return jnp.concatenate([x1 * cos - x2 * sin, x1 * sin + x2 * cos], axis=-1).astype(x.dtype)


def attend_prompt(q_lat, q_pe, c_kv, k_pe):
    B, L = q_lat.shape[0], q_lat.shape[1]
    nb = -(-L // Q_BLOCK)
    pad = nb * Q_BLOCK - L
    ql = jnp.pad(q_lat, ((0, 0), (0, pad), (0, 0), (0, 0)))
    qp = jnp.pad(q_pe, ((0, 0), (0, pad), (0, 0), (0, 0)))
    ql = ql.reshape(B, nb, Q_BLOCK, N_HEADS, KV_LORA).transpose(1, 0, 2, 3, 4)
    qp = qp.reshape(B, nb, Q_BLOCK, N_HEADS, QK_ROPE).transpose(1, 0, 2, 3, 4)
    starts = jnp.arange(nb, dtype=jnp.int32) * Q_BLOCK
    key_pos = jnp.arange(L, dtype=jnp.int32)

    def one_block(args):
        qlb, qpb, start = args
        s = jnp.einsum('bqhc,bkc->bhqk', qlb, c_kv) + jnp.einsum('bqhr,bkr->bhqk', qpb, k_pe)
        q_pos = start + jnp.arange(Q_BLOCK, dtype=jnp.int32)
        mask = key_pos[None, :] <= q_pos[:, None]
        s = jnp.where(mask, s.astype(jnp.float32) * ATTN_SCALE, -jnp.inf)
        p = jax.nn.softmax(s, axis=-1).astype(c_kv.dtype)
        return jnp.einsum('bhqk,bkc->bqhc', p, c_kv)

    o = lax.map(one_block, (ql, qp, starts))
    o = o.transpose(1, 0, 2, 3, 4).reshape(B, nb * Q_BLOCK, N_HEADS, KV_LORA)
    return o[:, :L]


def attend_sample(q_lat, q_pe, c_kv, k_pe, ckv_past, kpe_past):
    T = q_lat.shape[1]
    P = ckv_past.shape[1]
    s_past = jnp.einsum('bthc,bpc->bhtp', q_lat, ckv_past) + jnp.einsum('bthr,bpr->bhtp', q_pe, kpe_past)
    s_new = jnp.einsum('bthc,bsc->bhts', q_lat, c_kv) + jnp.einsum('bthr,bsr->bhts', q_pe, k_pe)
    causal = jnp.tril(jnp.ones((T, T), dtype=bool))
    s_new = jnp.where(causal, s_new.astype(jnp.float32) * ATTN_SCALE, -jnp.inf)
    s = jnp.concatenate([s_past.astype(jnp.float32) * ATTN_SCALE, s_new], axis=-1)
    p = jax.nn.softmax(s, axis=-1).astype(c_kv.dtype)
    return (jnp.einsum('bhtp,bpc->bthc', p[..., :P], ckv_past)
            + jnp.einsum('bhts,bsc->bthc', p[..., P:], c_kv))


def linear_scan(a, b, h0):
    b = b.at[:, 0].add(a[:, 0] * h0)

    def combine(c1, c2):
        a1, b1 = c1
        a2, b2 = c2
        return a1 * a2, a2 * b1 + b2

    _, h = lax.associative_scan(combine, (a, b), axis=1)
    return h


def rglru(u, g, conv_prev, h_prev, lp):
    B, T = u.shape[0], u.shape[1]
    xpad = jnp.concatenate([conv_prev.astype(u.dtype), u], axis=1)
    xc = lp["conv_b"]
    for k in range(CONV_W):
        xc = xc + xpad[:, k:k + T] * lp["conv_w"][k]
    new_conv = xpad[:, T:]
    xb = xc.reshape(B, T, N_RNN_BLOCKS, RNN_BLOCK)
    r = jax.nn.sigmoid(jnp.einsum('btnd,nde->btne', xb, lp["w_ra"]).reshape(B, T, D_RNN) + lp["b_ra"])
    i = jax.nn.sigmoid(jnp.einsum('btnd,nde->btne', xb, lp["w_ri"]).reshape(B, T, D_RNN) + lp["b_ri"])
    log_a = -LRU_C * r.astype(jnp.float32) * jax.nn.softplus(-lp["lru_a"].astype(jnp.float32))
    a = jnp.exp(log_a)
    b = jnp.sqrt(-jnp.expm1(2.0 * log_a)) * (i * xc).astype(jnp.float32)
    h = linear_scan(a, b, h_prev.astype(jnp.float32))
    y = h.astype(u.dtype) * jax.nn.gelu(g)
    return y, h[:, -1].astype(h_prev.dtype), new_conv


def layer_forward(x, pos, attend, conv_prev, h_prev, lp):
    B, T = x.shape[0], x.shape[1]
    xn = rmsnorm(x, lp["norm_mix"])
    c_q, c_kv, k_pe, u, g = jnp.split(xn @ lp["w_in"], SPLIT_AT, axis=-1)
    q = (rmsnorm(c_q, lp["q_norm"]) @ lp["w_uq"]).reshape(B, T, N_HEADS, QK_NOPE + QK_ROPE)
    q_lat = jnp.einsum('bthn,chn->bthc', q[..., :QK_NOPE], lp["w_uk"])
    q_pe = rope(q[..., QK_NOPE:], pos)
    c_kv = rmsnorm(c_kv, lp["kv_norm"])
    k_pe = rope(k_pe[:, :, None, :], pos)[:, :, 0]
    o_lat = attend(q_lat, q_pe, c_kv, k_pe)
    attn = jnp.einsum('bthc,chv->bthv', o_lat, lp["w_uv"]).reshape(B, T, N_HEADS * V_DIM)
    rnn, h_new, conv_new = rglru(u, g, conv_prev, h_prev, lp)
    x = x + jnp.concatenate([attn, rnn], axis=-1) @ lp["w_out"]
    xn = rmsnorm(x, lp["norm_ffn"])
    x = x + (jax.nn.silu(xn @ lp["w_gate"]) * (xn @ lp["w_up"])) @ lp["w_down"]
    return x, c_kv, k_pe, h_new, conv_new


def setup_inputs(seed: int = 0) -> dict:
    key = jax.random.key(seed)
    ks = jax.random.split(key, 32)
    f32 = jnp.float32

    def nrm(k, shape, s):
        return jax.random.normal(k, shape, f32) * s

    n_pages = PAST_LEN // PAGE_SIZE
    n_used = DEC_BATCH * n_pages
    n_phys = n_used + (n_used + 3) // 4
    page_table = jax.random.permutation(ks[0], n_phys)[:n_used].reshape(DEC_BATCH, n_pages).astype(jnp.int32)
    a0 = jax.random.uniform(ks[1], (DEPTH, D_RNN), f32, minval=0.9, maxval=0.999)
    s = a0 ** (1.0 / LRU_C)
    lru_a = jnp.log(s) - jnp.log1p(-s)
    return {
        "x_prompt": nrm(ks[2], (BATCH, SEQ, D_MODEL), 1.0),
        "x_sample": nrm(ks[3], (DEC_BATCH, DEC_SEQ, D_MODEL), 1.0),
        "cache_ckv": nrm(ks[4], (DEPTH, n_phys, PAGE_SIZE, KV_LORA), 1.0),
        "cache_kpe": nrm(ks[5], (DEPTH, n_phys, PAGE_SIZE, QK_ROPE), 1.0),
        "state_h": nrm(ks[6], (DEPTH, DEC_BATCH, D_RNN), 0.5),
        "state_conv": nrm(ks[7], (DEPTH, DEC_BATCH, CONV_W - 1, D_RNN), 1.0),
        "page_table": page_table,
        "meta_tokens": nrm(ks[8], (N_META, D_MODEL), 1.0),
        "norm_mix": 1.0 + nrm(ks[9], (DEPTH, D_MODEL), 0.02),
        "w_in": nrm(ks[10], (DEPTH, D_MODEL, IN_W), D_MODEL ** -0.5),
        "q_norm": 1.0 + nrm(ks[11], (DEPTH, Q_LORA), 0.02),
        "w_uq": nrm(ks[12], (DEPTH, Q_LORA, N_HEADS * (QK_NOPE + QK_ROPE)), Q_LORA ** -0.5),
        "kv_norm": 1.0 + nrm(ks[13], (DEPTH, KV_LORA), 0.02),
        "w_uk": nrm(ks[14], (DEPTH, KV_LORA, N_HEADS, QK_NOPE), KV_LORA ** -0.5),
        "w_uv": nrm(ks[15], (DEPTH, KV_LORA, N_HEADS, V_DIM), KV_LORA ** -0.5),
        "conv_w": nrm(ks[16], (DEPTH, CONV_W, D_RNN), CONV_W ** -0.5),
        "conv_b": nrm(ks[17], (DEPTH, D_RNN), 0.02),
        "w_ra": nrm(ks[18], (DEPTH, N_RNN_BLOCKS, RNN_BLOCK, RNN_BLOCK), RNN_BLOCK ** -0.5),
        "b_ra": nrm(ks[19], (DEPTH, D_RNN), 0.02),
        "w_ri": nrm(ks[20], (DEPTH, N_RNN_BLOCKS, RNN_BLOCK, RNN_BLOCK), RNN_BLOCK ** -0.5),
        "b_ri": nrm(ks[21], (DEPTH, D_RNN), 0.02),
        "lru_a": lru_a,
        "w_out": nrm(ks[22], (DEPTH, D_MIX, D_MODEL), D_MIX ** -0.5),
        "norm_ffn": 1.0 + nrm(ks[23], (DEPTH, D_MODEL), 0.02),
        "w_gate": nrm(ks[24], (DEPTH, D_MODEL, D_FF), D_MODEL ** -0.5),
        "w_up": nrm(ks[25], (DEPTH, D_MODEL, D_FF), D_MODEL ** -0.5),
        "w_down": nrm(ks[26], (DEPTH, D_FF, D_MODEL), D_FF ** -0.5),
        "final_norm": 1.0 + nrm(ks[27], (D_MODEL,), 0.02),
    }


def reference(x_prompt, x_sample, cache_ckv, cache_kpe, state_h, state_conv, page_table,
              meta_tokens, norm_mix, w_in, q_norm, w_uq, kv_norm, w_uk, w_uv, conv_w, conv_b,
              w_ra, b_ra, w_ri, b_ri, lru_a, w_out, norm_ffn, w_gate, w_up, w_down, final_norm):
    B = x_prompt.shape[0]
    DB, T = x_sample.shape[0], x_sample.shape[1]
    meta = jnp.broadcast_to(meta_tokens[None].astype(x_prompt.dtype), (B, N_META, D_MODEL))
    xp = jnp.concatenate([meta, x_prompt], axis=1)
    L = xp.shape[1]
    pos_p = jnp.arange(L, dtype=jnp.float32)
    pos_s = PAST_LEN + jnp.arange(T, dtype=jnp.float32)
    conv0 = jnp.zeros((B, CONV_W - 1, D_RNN), x_prompt.dtype)
    h0 = jnp.zeros((B, D_RNN), state_h.dtype)
    xs = x_sample
    ckv_p, kpe_p, hh_p, cv_p = [], [], [], []
    ckv_s, kpe_s, hh_s, cv_s = [], [], [], []
    for l in range(DEPTH):
        lp = {"norm_mix": norm_mix[l], "w_in": w_in[l], "q_norm": q_norm[l], "w_uq": w_uq[l],
              "kv_norm": kv_norm[l], "w_uk": w_uk[l], "w_uv": w_uv[l], "conv_w": conv_w[l],
              "conv_b": conv_b[l], "w_ra": w_ra[l], "b_ra": b_ra[l], "w_ri": w_ri[l], "b_ri": b_ri[l],
              "lru_a": lru_a[l], "w_out": w_out[l], "norm_ffn": norm_ffn[l], "w_gate": w_gate[l],
              "w_up": w_up[l], "w_down": w_down[l]}
        xp, c1, k1, h1, v1 = layer_forward(xp, pos_p, attend_prompt, conv0, h0, lp)
        ckv_p.append(c1); kpe_p.append(k1); hh_p.append(h1); cv_p.append(v1)
        ckv_past = cache_ckv[l, page_table].reshape(DB, -1, KV_LORA)
        kpe_past = cache_kpe[l, page_table].reshape(DB, -1, QK_ROPE)
        att_s = functools.partial(attend_sample, ckv_past=ckv_past, kpe_past=kpe_past)
        xs, c2, k2, h2, v2 = layer_forward(xs, pos_s, att_s, state_conv[l], state_h[l], lp)
        ckv_s.append(c2); kpe_s.append(k2); hh_s.append(h2); cv_s.append(v2)
    y_prompt = rmsnorm(xp, final_norm)[:, N_META:]
    y_sample = rmsnorm(xs, final_norm)
    return (y_prompt, y_sample,
            jnp.stack(ckv_p), jnp.stack(kpe_p), jnp.stack(hh_p), jnp.stack(cv_p),
            jnp.stack(ckv_s), jnp.stack(kpe_s), jnp.stack(hh_s), jnp.stack(cv_s))
```

```python
import functools
import math

import jax
import jax.numpy as jnp
from jax import lax
from jax.experimental import pallas as pl
from jax.experimental.pallas import tpu as pltpu

F32 = jnp.float32
BF16 = jnp.bfloat16

D_MODEL = 1024
N_META = 16
N_HEADS = 8
QK_NOPE = 64
QK_ROPE = 32
V_DIM = 64
Q_LORA = 384
KV_LORA = 256
ROPE_THETA = 10000.0
D_RNN = 512
N_RNN_BLOCKS = 8
RNN_BLOCK = D_RNN // N_RNN_BLOCKS
CONV_W = 4
LRU_C = 8.0
D_FF = 2816
RMS_EPS = 1e-6
PAGE_SIZE = 128
ATTN_SCALE = 1.0 / math.sqrt(QK_NOPE + QK_ROPE)

LANES = 128
SUBLANES = 8
VMEM_LIMIT = 56 * 1024 * 1024

OFF_CQ = 0
OFF_CKV = OFF_CQ + Q_LORA
OFF_U = OFF_CKV + KV_LORA
OFF_G = OFF_U + D_RNN
OFF_KPA = OFF_G + D_RNN
OFF_KPB = OFF_KPA + LANES
IN_COLS = OFF_KPB + LANES
QAB_COLS = 2 * N_HEADS * LANES
KV_COLS = N_HEADS * LANES + N_HEADS * V_DIM

META_ROWS = 128
META_PAD = META_ROWS - N_META
ATTN_TQ = 256
FF_CHUNK = 256
DEC_PAGES = 32
DEC_SUB = 1024


def _rms(x, g):
    return x * lax.rsqrt(jnp.mean(x * x, axis=-1, keepdims=True) + RMS_EPS) * g


def _dot(a, b):
    return jnp.dot(a, b, preferred_element_type=F32)


def _dot_nt(a, b):
    return lax.dot_general(a, b, (((1,), (1,)), ((), ())), preferred_element_type=F32)


def _softplus(z):
    return jnp.maximum(z, 0.0) + jnp.log1p(jnp.exp(-jnp.abs(z)))


def _gelu_tanh(x):
    cdf = 0.5 * (1.0 + jnp.tanh(math.sqrt(2.0 / math.pi) * (x + 0.044715 * (x * x * x))))
    return x * cdf


def _const_spec(shape):
    zeros = (0,) * len(shape)
    return pl.BlockSpec(shape, lambda *_: zeros, pipeline_mode=pl.Buffered(1))


def _front(x_ref, tab_ref, nm_ref, win_ref, qn_ref, wuq_ref, kvn_ref, proj_s, qab_s):
    xn = _rms(x_ref[...], nm_ref[...]).astype(BF16)
    proj_s[...] = _dot(xn, win_ref[...])
    cqn = _rms(proj_s[:, OFF_CQ:OFF_CQ + Q_LORA], qn_ref[...]).astype(BF16)
    qab_s[...] = _dot(cqn, wuq_ref[...])
    ckv_n = _rms(proj_s[:, OFF_CKV:OFF_CKV + KV_LORA], kvn_ref[...])
    ck_t = tab_ref[:, 2 * LANES:3 * LANES]
    sk_t = tab_ref[:, 3 * LANES:4 * LANES]
    kpe = proj_s[:, OFF_KPA:OFF_KPA + LANES] * ck_t + proj_s[:, OFF_KPB:OFF_KPB + LANES] * sk_t
    return ckv_n, kpe


def _q_head(qab_s, tab_ref, h):
    cq_t = tab_ref[:, 0:LANES]
    sq_t = tab_ref[:, LANES:2 * LANES]
    lo = h * LANES
    hi = N_HEADS * LANES + h * LANES
    return qab_s[:, lo:lo + LANES] * cq_t + qab_s[:, hi:hi + LANES] * sq_t


def _lru_terms(xc, wg_ref, bg_ref, la_ref):
    gates = _dot(xc.astype(BF16), wg_ref[...]) + bg_ref[...]
    r = jax.nn.sigmoid(gates[:, 0:D_RNN])
    i = jax.nn.sigmoid(gates[:, D_RNN:2 * D_RNN])
    log_a = (-LRU_C * r) * _softplus(-la_ref[...])
    a = jnp.exp(log_a)
    mult = jnp.sqrt(-jnp.tanh(log_a) * (a * a + 1.0))
    return a, mult * (i * xc)


def _in_prompt_kernel(x_ref, tab_ref, h0_ref, c0_ref,
                      nm_ref, win_ref, qn_ref, wuq_ref, kvn_ref, wkv_ref,
                      cw_ref, cb_ref, wg_ref, bg_ref, la_ref,
                      ckv_ref, kpe_ref, q_ref, k_ref, v_ref, rnn_ref, hf_ref, cf_ref,
                      proj_s, qab_s, kv_s, ubuf, a_s, b_s, h_s, hc_s, *, tm, n_pad):
    t = pl.program_id(1)
    ckv_n, kpe = _front(x_ref, tab_ref, nm_ref, win_ref, qn_ref, wuq_ref, kvn_ref, proj_s, qab_s)
    ckv_ref[...] = ckv_n
    kpe_ref[...] = kpe
    kv_s[...] = _dot(ckv_n.astype(BF16), wkv_ref[...])
    for h in range(N_HEADS):
        sl = slice(h * LANES, (h + 1) * LANES)
        q_ref[:, sl] = _q_head(qab_s, tab_ref, h).astype(BF16)
        k_ref[:, sl] = (kv_s[:, sl] + kpe).astype(BF16)
    v_ref[...] = kv_s[:, N_HEADS * LANES:KV_COLS].astype(BF16)

    @pl.when(t == 0)
    def _():
        ubuf[0:SUBLANES, :] = c0_ref[...]
        hc_s[...] = h0_ref[...]

    @pl.when(t > 0)
    def _():
        ubuf[0:SUBLANES, :] = ubuf[tm:tm + SUBLANES, :]

    u = proj_s[:, OFF_U:OFF_U + D_RNN]
    ubuf[SUBLANES:SUBLANES + tm, :] = u
    cw = cw_ref[...]
    xc = cb_ref[...]
    for k in range(CONV_W - 1):
        lo = SUBLANES - (CONV_W - 1) + k
        xc = xc + ubuf[lo:lo + tm, :] * cw[k:k + 1, :]
    xc = xc + u * cw[CONV_W - 1:CONV_W, :]
    a, b = _lru_terms(xc, wg_ref, bg_ref, la_ref)
    row = lax.broadcasted_iota(jnp.int32, (tm, D_RNN), 0)
    if n_pad:
        b = jnp.where(row + t * tm >= n_pad, b, 0.0)

    sub = row & (SUBLANES - 1)
    for d in (1, 2, 4):
        keep = sub >= d
        b = jnp.where(keep, a * pltpu.roll(b, d, 0) + b, b)
        a = jnp.where(keep, a * pltpu.roll(a, d, 0), a)
    a_s[...] = a
    b_s[...] = b

    def group(gi, h_in):
        o = pl.multiple_of(gi * SUBLANES, SUBLANES)
        hr = a_s[pl.ds(o, SUBLANES), :] * h_in + b_s[pl.ds(o, SUBLANES), :]
        h_s[pl.ds(o, SUBLANES), :] = hr
        return hr[SUBLANES - 1:SUBLANES, :]

    h_last = lax.fori_loop(0, tm // SUBLANES, group, hc_s[...], unroll=8)
    hc_s[...] = h_last
    hf_ref[...] = h_last
    cf_ref[...] = ubuf[tm:tm + SUBLANES, :]
    rnn_ref[...] = (h_s[...] * _gelu_tanh(proj_s[:, OFF_G:OFF_G + D_RNN])).astype(BF16)


def _in_prompt(x, tab, h0, c0, w, *, tm, n_pad):
    nb, L, _ = x.shape
    nt = L // tm
    assert nt * tm == L
    bstate = (lambda b, t: (b, 0, 0)) if h0.shape[0] == nb else (lambda b, t: (0, 0, 0))
    row_spec = lambda c: pl.BlockSpec((None, tm, c), lambda b, t: (b, t, 0))
    in_specs = [
        row_spec(D_MODEL),
        pl.BlockSpec((tm, 4 * LANES), lambda b, t: (t, 0)),
        pl.BlockSpec((None, 1, D_RNN), bstate),
        pl.BlockSpec((None, SUBLANES, D_RNN), bstate),
        _const_spec((1, D_MODEL)), _const_spec((D_MODEL, IN_COLS)),
        _const_spec((1, Q_LORA)), _const_spec((Q_LORA, QAB_COLS)),
        _const_spec((1, KV_LORA)), _const_spec((KV_LORA, KV_COLS)),
        _const_spec((CONV_W, D_RNN)), _const_spec((1, D_RNN)),
        _const_spec((D_RNN, 2 * D_RNN)), _const_spec((1, 2 * D_RNN)), _const_spec((1, D_RNN)),
    ]
    out_shape = [
        jax.ShapeDtypeStruct((nb, L, KV_LORA), F32),
        jax.ShapeDtypeStruct((nb, L, LANES), F32),
        jax.ShapeDtypeStruct((nb, L, N_HEADS * LANES), BF16),
        jax.ShapeDtypeStruct((nb, L, N_HEADS * LANES), BF16),
        jax.ShapeDtypeStruct((nb, L, N_HEADS * V_DIM), BF16),
        jax.ShapeDtypeStruct((nb, L, D_RNN), BF16),
        jax.ShapeDtypeStruct((nb, 1, D_RNN), F32),
        jax.ShapeDtypeStruct((nb, SUBLANES, D_RNN), F32),
    ]
    out_specs = [
        row_spec(KV_LORA), row_spec(LANES), row_spec(N_HEADS * LANES), row_spec(N_HEADS * LANES),
        row_spec(N_HEADS * V_DIM), row_spec(D_RNN),
        pl.BlockSpec((None, 1, D_RNN), lambda b, t: (b, 0, 0)),
        pl.BlockSpec((None, SUBLANES, D_RNN), lambda b, t: (b, 0, 0)),
    ]
    scratch = [
        pltpu.VMEM((tm, IN_COLS), F32), pltpu.VMEM((tm, QAB_COLS), F32), pltpu.VMEM((tm, KV_COLS), F32),
        pltpu.VMEM((tm + SUBLANES, D_RNN), F32),
        pltpu.VMEM((tm, D_RNN), F32), pltpu.VMEM((tm, D_RNN), F32), pltpu.VMEM((tm, D_RNN), F32),
        pltpu.VMEM((1, D_RNN), F32),
    ]
    return pl.pallas_call(
        functools.partial(_in_prompt_kernel, tm=tm, n_pad=n_pad),
        grid=(nb, nt), in_specs=in_specs, out_specs=out_specs, out_shape=out_shape,
        scratch_shapes=scratch, name="in_prompt",
        compiler_params=pltpu.CompilerParams(
            dimension_semantics=("arbitrary", "arbitrary"), vmem_limit_bytes=VMEM_LIMIT),
    )(x, tab, h0, c0, w["norm_mix"], w["w_in"], w["q_norm"], w["w_uq"], w["kv_norm"], w["w_kv"],
      w["conv_w"], w["conv_b"], w["w_gates"], w["b_gates"], w["lru_a"])


def _in_sample_kernel(x_ref, tab_ref, h0_ref, cp_ref,
                      nm_ref, win_ref, qn_ref, wuq_ref, kvn_ref, wukt_ref,
                      cw_ref, cb_ref, wg_ref, bg_ref, la_ref,
                      ckv_ref, kpe_ref, qlat_ref, qrot_ref, rnn_ref, hf_ref, cf_ref,
                      proj_s, qab_s, xc_s, h_s, *, nb, nt):
    ckv_n, kpe = _front(x_ref, tab_ref, nm_ref, win_ref, qn_ref, wuq_ref, kvn_ref, proj_s, qab_s)
    ckv_ref[...] = ckv_n
    kpe_ref[...] = kpe
    for h in range(N_HEADS):
        qh = _q_head(qab_s, tab_ref, h).astype(BF16)
        qrot_ref[:, h * LANES:(h + 1) * LANES] = qh
        qlat_ref[h] = _dot(qh, wukt_ref[h]).astype(BF16)

    cw = cw_ref[...]

    def xpad(j):
        if j < CONV_W - 1:
            return cp_ref[j]
        jj = j - (CONV_W - 1)
        return proj_s[jj * nb:(jj + 1) * nb, OFF_U:OFF_U + D_RNN]

    for t in range(nt):
        xc = cb_ref[...]
        for k in range(CONV_W):
            xc = xc + xpad(t + k) * cw[k:k + 1, :]
        xc_s[t * nb:(t + 1) * nb, :] = xc
    a, b = _lru_terms(xc_s[...], wg_ref, bg_ref, la_ref)
    h = h0_ref[...]
    for t in range(nt):
        h = a[t * nb:(t + 1) * nb, :] * h + b[t * nb:(t + 1) * nb, :]
        h_s[t * nb:(t + 1) * nb, :] = h
    hf_ref[...] = h
    for j in range(CONV_W - 1):
        cf_ref[j] = xpad(nt + j)
    rnn_ref[...] = (h_s[...] * _gelu_tanh(proj_s[:, OFF_G:OFF_G + D_RNN])).astype(BF16)


def _in_sample(x, tab, h0, cp, w, *, nb, nt):
    n = nb * nt
    full = lambda shape: pl.BlockSpec(shape, lambda i: (0,) * len(shape))
    in_specs = [
        full((n, D_MODEL)), full((n, 4 * LANES)), full((nb, D_RNN)), full((CONV_W - 1, nb, D_RNN)),
        full((1, D_MODEL)), full((D_MODEL, IN_COLS)), full((1, Q_LORA)), full((Q_LORA, QAB_COLS)),
        full((1, KV_LORA)), full((N_HEADS, LANES, KV_LORA)),
        full((CONV_W, D_RNN)), full((1, D_RNN)), full((D_RNN, 2 * D_RNN)), full((1, 2 * D_RNN)),
        full((1, D_RNN)),
    ]
    out_shape = [
        jax.ShapeDtypeStruct((n, KV_LORA), F32),
        jax.ShapeDtypeStruct((n, LANES), F32),
        jax.ShapeDtypeStruct((N_HEADS, n, KV_LORA), BF16),
        jax.ShapeDtypeStruct((n, N_HEADS * LANES), BF16),
        jax.ShapeDtypeStruct((n, D_RNN), BF16),
        jax.ShapeDtypeStruct((nb, D_RNN), F32),
        jax.ShapeDtypeStruct((CONV_W - 1, nb, D_RNN), F32),
    ]
    out_specs = [full(s.shape) for s in out_shape]
    scratch = [
        pltpu.VMEM((n, IN_COLS), F32), pltpu.VMEM((n, QAB_COLS), F32),
        pltpu.VMEM((n, D_RNN), F32), pltpu.VMEM((n, D_RNN), F32),
    ]
    return pl.pallas_call(
        functools.partial(_in_sample_kernel, nb=nb, nt=nt),
        grid=(1,), in_specs=in_specs, out_specs=out_specs, out_shape=out_shape,
        scratch_shapes=scratch, name="in_sample",
        compiler_params=pltpu.CompilerParams(
            dimension_semantics=("arbitrary",), vmem_limit_bytes=VMEM_LIMIT),
    )(x, tab, h0, cp, w["norm_mix"], w["w_in"], w["q_norm"], w["w_uq"], w["kv_norm"], w["w_ukt"],
      w["conv_w"], w["conv_b"], w["w_gates"], w["b_gates"], w["lru_a"])


def _softmax_step(state, s, v):
    m, l, acc = state
    m_new = jnp.maximum(m, jnp.max(s, axis=-1, keepdims=True))
    alpha = jnp.exp(m - m_new)
    p = jnp.exp(s - m_new)
    l = alpha * l + jnp.sum(p, axis=-1, keepdims=True)
    acc = alpha * acc + _dot(p.astype(BF16), v)
    return m_new, l, acc


def _softmax_init(s, v):
    m = jnp.max(s, axis=-1, keepdims=True)
    p = jnp.exp(s - m)
    return m, jnp.sum(p, axis=-1, keepdims=True), _dot(p.astype(BF16), v)


def _merge_pair(st0, st1):
    o0 = st0[2] / st0[1]
    o1 = st1[2] / st1[1]
    lane = lax.broadcasted_iota(jnp.int32, o0.shape, 1)
    return jnp.where(lane < V_DIM, o0, o1).astype(BF16)


def _attn_meta_kernel(q_ref, k_ref, v_ref, o_ref, *, n_pad):
    n = q_ref.shape[0]
    row = lax.broadcasted_iota(jnp.int32, (n, n), 0)
    col = lax.broadcasted_iota(jnp.int32, (n, n), 1)
    mask = (col <= row) & ((col >= n_pad) | (row < n_pad))
    v = v_ref[...]
    sts = []
    for hh in range(2):
        sl = slice(hh * LANES, (hh + 1) * LANES)
        s = jnp.where(mask, _dot_nt(q_ref[:, sl], k_ref[:, sl]), -jnp.inf)
        sts.append(_softmax_init(s, v))
    o_ref[...] = _merge_pair(*sts)


def _attn_meta(q, k, v, *, n_pad):
    n = q.shape[0]
    return pl.pallas_call(
        functools.partial(_attn_meta_kernel, n_pad=n_pad),
        grid=(N_HEADS // 2,),
        in_specs=[pl.BlockSpec((n, 2 * LANES), lambda p: (0, p)),
                  pl.BlockSpec((n, 2 * LANES), lambda p: (0, p)),
                  pl.BlockSpec((n, LANES), lambda p: (0, p))],
        out_specs=pl.BlockSpec((n, LANES), lambda p: (0, p)),
        out_shape=jax.ShapeDtypeStruct((n, N_HEADS * V_DIM), BF16),
        name="attn_meta",
        compiler_params=pltpu.CompilerParams(dimension_semantics=("arbitrary",)),
    )(q, k, v)


def _attn_main_kernel(q_ref, k_ref, v_ref, kp_ref, vp_ref, o_ref, *, tq, n_pad):
    L = q_ref.shape[0]
    npre = kp_ref.shape[0]
    pre_ok = lax.broadcasted_iota(jnp.int32, (tq, npre), 1) >= n_pad
    causal = (lax.broadcasted_iota(jnp.int32, (tq, tq), 1)
              <= lax.broadcasted_iota(jnp.int32, (tq, tq), 0))

    def qblock(qb, carry):
        r0 = pl.multiple_of(qb * tq, tq)
        qs = [q_ref[pl.ds(r0, tq), hh * LANES:(hh + 1) * LANES] for hh in range(2)]
        vp = vp_ref[...]
        sts = []
        for hh in range(2):
            s = _dot_nt(qs[hh], kp_ref[:, hh * LANES:(hh + 1) * LANES])
            sts.append(_softmax_init(jnp.where(pre_ok, s, -jnp.inf), vp))

        def kvstep(j, sts):
            c0 = pl.multiple_of(j * tq, tq)
            v = v_ref[pl.ds(c0, tq), :]
            out = []
            for hh in range(2):
                s = _dot_nt(qs[hh], k_ref[pl.ds(c0, tq), hh * LANES:(hh + 1) * LANES])
                out.append(_softmax_step(sts[hh], s, v))
            return tuple(out)

        sts = lax.fori_loop(0, qb, kvstep, tuple(sts))
        v = v_ref[pl.ds(r0, tq), :]
        fin = []
        for hh in range(2):
            s = _dot_nt(qs[hh], k_ref[pl.ds(r0, tq), hh * LANES:(hh + 1) * LANES])
            fin.append(_softmax_step(sts[hh], jnp.where(causal, s, -jnp.inf), v))
        o_ref[pl.ds(r0, tq), :] = _merge_pair(*fin)
        return carry

    lax.fori_loop(0, L // tq, qblock, 0)


def _attn_main(q, k, v, kpre, vpre, *, n_pad):
    B, L, _ = q.shape
    npre = kpre.shape[0]
    tq = min(ATTN_TQ, L)
    assert L % tq == 0
    return pl.pallas_call(
        functools.partial(_attn_main_kernel, tq=tq, n_pad=n_pad),
        grid=(B, N_HEADS // 2),
        in_specs=[pl.BlockSpec((None, L, 2 * LANES), lambda b, p: (b, 0, p)),
                  pl.BlockSpec((None, L, 2 * LANES), lambda b, p: (b, 0, p)),
                  pl.BlockSpec((None, L, LANES), lambda b, p: (b, 0, p)),
                  pl.BlockSpec((npre, 2 * LANES), lambda b, p: (0, p)),
                  pl.BlockSpec((npre, LANES), lambda b, p: (0, p))],
        out_specs=pl.BlockSpec((None, L, LANES), lambda b, p: (b, 0, p)),
        out_shape=jax.ShapeDtypeStruct((B, L, N_HEADS * V_DIM), BF16),
        name="attn_main",
        compiler_params=pltpu.CompilerParams(
            dimension_semantics=("arbitrary", "arbitrary"), vmem_limit_bytes=VMEM_LIMIT),
    )(q, k, v, kpre, vpre)


def _decode_kernel(pt_ref, ql_ref, qp_ref, cn_ref, kn_ref, ckv_hbm, kpe_hbm, o_ref,
                   cbuf, kbuf, ncbuf, nkbuf, sem, m_s, l_s, acc_s, *, layer, G, NC, T, sub):
    b = pl.program_id(0)
    c = pl.program_id(1)
    n = b * NC + c
    total = pl.num_programs(0) * NC
    slot = lax.rem(n, 2)

    def copies(bb, cc, sl):
        out = []
        for g in range(G):
            page = pt_ref[bb, cc * G + g]
            out.append(pltpu.make_async_copy(ckv_hbm.at[layer, page], cbuf.at[sl, g], sem.at[0, sl]))
            out.append(pltpu.make_async_copy(kpe_hbm.at[layer, page], kbuf.at[sl, g], sem.at[1, sl]))
        return out

    @pl.when(n == 0)
    def _():
        ncbuf[...] = jnp.zeros_like(ncbuf)
        nkbuf[...] = jnp.zeros_like(nkbuf)
        for cp in copies(b, c, slot):
            cp.start()

    @pl.when(n + 1 < total)
    def _():
        n1 = n + 1
        for cp in copies(n1 // NC, lax.rem(n1, NC), 1 - slot):
            cp.start()

    for cp in copies(b, c, slot):
        cp.wait()

    ql = ql_ref[...]
    qp = qp_ref[...]
    rows = ql.shape[0]

    @pl.when(c == 0)
    def _():
        m_s[...] = jnp.full_like(m_s, -jnp.inf)
        l_s[...] = jnp.zeros_like(l_s)
        acc_s[...] = jnp.zeros_like(acc_s)

    st = (m_s[...], l_s[...], acc_s[...])
    pp = sub // PAGE_SIZE
    for sc in range(G // pp):
        kv = cbuf[slot, sc * pp:(sc + 1) * pp].reshape(sub, KV_LORA).astype(BF16)
        kp = kbuf[slot, sc * pp:(sc + 1) * pp].reshape(sub, QK_ROPE).astype(BF16)
        s = _dot_nt(ql, kv) + _dot_nt(qp, kp)
        st = _softmax_step(st, s, kv)
    m_s[...], l_s[...], acc_s[...] = st

    @pl.when(c == NC - 1)
    def _():
        ncbuf[0:T, :] = cn_ref[...]
        nkbuf[0:T, :] = kn_ref[...]
        kv = ncbuf[...].astype(BF16)
        kp = nkbuf[...].astype(BF16)
        s = _dot_nt(ql, kv) + _dot_nt(qp, kp)
        tq = lax.rem(lax.broadcasted_iota(jnp.int32, s.shape, 0), T)
        col = lax.broadcasted_iota(jnp.int32, s.shape, 1)
        s = jnp.where(col <= tq, s, -jnp.inf)
        m, l, acc = _softmax_step((m_s[...], l_s[...], acc_s[...]), s, kv)
        o_ref[...] = acc / l


def _decode_attn(page_table, qlat, qpe, ckv_new, kpe_new, cache_ckv, cache_kpe, *, layer):
    DB, rows, _ = qlat.shape
    T = ckv_new.shape[1]
    n_pages = page_table.shape[1]
    G = min(DEC_PAGES, n_pages)
    assert n_pages % G == 0
    NC = n_pages // G
    sub = min(DEC_SUB, G * PAGE_SIZE)
    assert (G * PAGE_SIZE) % sub == 0
    grid_spec = pltpu.PrefetchScalarGridSpec(
        num_scalar_prefetch=1,
        grid=(DB, NC),
        in_specs=[
            pl.BlockSpec((None, rows, KV_LORA), lambda b, c, pt: (b, 0, 0)),
            pl.BlockSpec((None, rows, QK_ROPE), lambda b, c, pt: (b, 0, 0)),
            pl.BlockSpec((None, T, KV_LORA), lambda b, c, pt: (b, 0, 0)),
            pl.BlockSpec((None, T, QK_ROPE), lambda b, c, pt: (b, 0, 0)),
            pl.BlockSpec(memory_space=pl.ANY),
            pl.BlockSpec(memory_space=pl.ANY),
        ],
        out_specs=pl.BlockSpec((None, rows, KV_LORA), lambda b, c, pt: (b, 0, 0)),
        scratch_shapes=[
            pltpu.VMEM((2, G, PAGE_SIZE, KV_LORA), F32),
            pltpu.VMEM((2, G, PAGE_SIZE, QK_ROPE), F32),
            pltpu.VMEM((PAGE_SIZE, KV_LORA), F32),
            pltpu.VMEM((PAGE_SIZE, QK_ROPE), F32),
            pltpu.SemaphoreType.DMA((2, 2)),
            pltpu.VMEM((rows, 1), F32), pltpu.VMEM((rows, 1), F32), pltpu.VMEM((rows, KV_LORA), F32),
        ],
    )
    return pl.pallas_call(
        functools.partial(_decode_kernel, layer=layer, G=G, NC=NC, T=T, sub=sub),
        grid_spec=grid_spec,
        out_shape=jax.ShapeDtypeStruct((DB, rows, KV_LORA), F32),
        name="decode_attn",
        compiler_params=pltpu.CompilerParams(
            dimension_semantics=("arbitrary", "arbitrary"), vmem_limit_bytes=VMEM_LIMIT),
    )(page_table, qlat, qpe, ckv_new, kpe_new, cache_ckv, cache_kpe)


def _ffn_tail(x1, nf_ref, wg_ref, wu_ref, wd_ref, fn_ref, o_ref, acc_s, final):
    xn = _rms(x1, nf_ref[...]).astype(BF16)
    for ci in range(D_FF // FF_CHUNK):
        sl = slice(ci * FF_CHUNK, (ci + 1) * FF_CHUNK)
        hg = _dot(xn, wg_ref[:, sl])
        hu = _dot(xn, wu_ref[:, sl])
        act = ((hg * jax.nn.sigmoid(hg)) * hu).astype(BF16)
        part = _dot(act, wd_ref[sl, :])
        if ci == 0:
            acc_s[...] = part
        else:
            acc_s[...] += part
    x2 = x1 + acc_s[...]
    if final:
        x2 = _rms(x2, fn_ref[...])
    o_ref[...] = x2


def _out_kernel(x_ref, a_ref, r_ref, wo_ref, nf_ref, wg_ref, wu_ref, wd_ref, fn_ref, o_ref, acc_s,
                *, final):
    na = a_ref.shape[1]
    y = _dot(a_ref[...], wo_ref[0:na, :]) + _dot(r_ref[...], wo_ref[na:, :])
    _ffn_tail(x_ref[...] + y, nf_ref, wg_ref, wu_ref, wd_ref, fn_ref, o_ref, acc_s, final)


def _out_latent_kernel(x_ref, ol_ref, wuv_ref, r_ref, wo_ref, nf_ref, wg_ref, wu_ref, wd_ref, fn_ref,
                       o_ref, acc_s, *, final):
    na = N_HEADS * V_DIM
    y = _dot(r_ref[...], wo_ref[na:, :])
    for h in range(N_HEADS):
        ah = _dot(ol_ref[h], wuv_ref[h]).astype(BF16)
        y = y + _dot(ah, wo_ref[h * V_DIM:(h + 1) * V_DIM, :])
    _ffn_tail(x_ref[...] + y, nf_ref, wg_ref, wu_ref, wd_ref, fn_ref, o_ref, acc_s, final)


def _out_proj(x, attn, rnn, w, final_norm, *, tm, final, latent=False):
    n = x.shape[0]
    assert n % tm == 0
    row = lambda c: pl.BlockSpec((tm, c), lambda i: (i, 0))
    wspecs = [_const_spec((D_MODEL, D_MODEL)), _const_spec((1, D_MODEL)),
              _const_spec((D_MODEL, D_FF)), _const_spec((D_MODEL, D_FF)), _const_spec((D_FF, D_MODEL)),
              _const_spec((1, D_MODEL))]
    wargs = (w["w_out"], w["norm_ffn"], w["w_gate"], w["w_up"], w["w_down"], final_norm)
    if latent:
        kern = functools.partial(_out_latent_kernel, final=final)
        in_specs = [row(D_MODEL), pl.BlockSpec((N_HEADS, tm, KV_LORA), lambda i: (0, i, 0)),
                    _const_spec((N_HEADS, KV_LORA, V_DIM)), row(D_RNN)] + wspecs
        args = (x, attn, w["w_uv3"], rnn) + wargs
    else:
        kern = functools.partial(_out_kernel, final=final)
        in_specs = [row(D_MODEL), row(N_HEADS * V_DIM), row(D_RNN)] + wspecs
        args = (x, attn, rnn) + wargs
    return pl.pallas_call(
        kern, grid=(n // tm,), in_specs=in_specs, out_specs=row(D_MODEL),
        out_shape=jax.ShapeDtypeStruct((n, D_MODEL), F32),
        scratch_shapes=[pltpu.VMEM((tm, D_MODEL), F32)],
        name="out_latent" if latent else "out_proj",
        compiler_params=pltpu.CompilerParams(
            dimension_semantics=("arbitrary",), vmem_limit_bytes=VMEM_LIMIT),
    )(*args)


def _rot_half_cols(w):
    half = QK_ROPE // 2
    return jnp.concatenate([-w[..., half:], w[..., :half]], axis=-1)


def _prep_layer(l, p):
    w_in = p["w_in"][l]
    d = w_in.shape[0]
    o_kpe = Q_LORA + KV_LORA
    kpe = w_in[:, o_kpe:o_kpe + QK_ROPE]
    z = lambda c: jnp.zeros((d, c), F32)
    w_in_r = jnp.concatenate(
        [w_in[:, :o_kpe], w_in[:, o_kpe + QK_ROPE:],
         z(QK_NOPE), kpe, z(LANES - QK_NOPE - QK_ROPE),
         z(QK_NOPE), _rot_half_cols(kpe), z(LANES - QK_NOPE - QK_ROPE)], axis=1)

    w_uq = p["w_uq"][l].reshape(Q_LORA, N_HEADS, QK_NOPE + QK_ROPE)
    nope, pe = w_uq[..., :QK_NOPE], w_uq[..., QK_NOPE:]
    zq = lambda c: jnp.zeros((Q_LORA, N_HEADS, c), F32)
    qa = jnp.concatenate([nope, pe, zq(LANES - QK_NOPE - QK_ROPE)], axis=-1)
    qb = jnp.concatenate([zq(QK_NOPE), _rot_half_cols(pe), zq(LANES - QK_NOPE - QK_ROPE)], axis=-1)
    w_uq_r = jnp.concatenate([qa.reshape(Q_LORA, -1), qb.reshape(Q_LORA, -1)], axis=1)

    w_uk = p["w_uk"][l]
    w_uv = p["w_uv"][l]
    w_k = jnp.concatenate([w_uk, jnp.zeros((KV_LORA, N_HEADS, LANES - QK_NOPE), F32)], axis=-1)
    w_kv = jnp.concatenate([w_k.reshape(KV_LORA, -1), w_uv.reshape(KV_LORA, -1)], axis=1)
    w_ukt = jnp.concatenate(
        [jnp.transpose(w_uk, (1, 2, 0)), jnp.zeros((N_HEADS, LANES - QK_NOPE, KV_LORA), F32)], axis=1)

    eye = jnp.eye(N_RNN_BLOCKS, dtype=F32)
    bd = lambda wb: jnp.einsum("nde,nm->ndme", wb, eye).reshape(D_RNN, D_RNN)
    w_gates = jnp.concatenate([bd(p["w_ra"][l]), bd(p["w_ri"][l])], axis=1)
    row = lambda v: v.reshape(1, -1).astype(F32)
    return {
        "norm_mix": row(p["norm_mix"][l]), "w_in": w_in_r.astype(BF16),
        "q_norm": row(p["q_norm"][l]), "w_uq": w_uq_r.astype(BF16),
        "kv_norm": row(p["kv_norm"][l]), "w_kv": w_kv.astype(BF16), "w_ukt": w_ukt.astype(BF16),
        "w_uv3": jnp.transpose(w_uv, (1, 0, 2)).astype(BF16),
        "conv_w": p["conv_w"][l].astype(F32), "conv_b": row(p["conv_b"][l]),
        "w_gates": w_gates.astype(BF16),
        "b_gates": jnp.concatenate([p["b_ra"][l], p["b_ri"][l]]).reshape(1, -1).astype(F32),
        "lru_a": row(p["lru_a"][l]),
        "w_out": p["w_out"][l].astype(BF16), "norm_ffn": row(p["norm_ffn"][l]),
        "w_gate": p["w_gate"][l].astype(BF16), "w_up": p["w_up"][l].astype(BF16),
        "w_down": p["w_down"][l].astype(BF16),
    }


def _rope_table(pos):
    half = QK_ROPE // 2
    inv = ROPE_THETA ** (-jnp.arange(half, dtype=F32) / half)
    ang = pos[:, None] * inv[None, :]
    cos = jnp.tile(jnp.cos(ang), (1, 2))
    sin = jnp.tile(jnp.sin(ang), (1, 2))
    n = pos.shape[0]
    one = jnp.ones((n, QK_NOPE), F32)
    z_lo = jnp.zeros((n, QK_NOPE), F32)
    z_hi = jnp.zeros((n, LANES - QK_NOPE - QK_ROPE), F32)
    return jnp.concatenate(
        [one * ATTN_SCALE, cos * ATTN_SCALE, z_hi, z_lo, sin * ATTN_SCALE, z_hi,
         z_lo, cos, z_hi, z_lo, sin, z_hi], axis=1)


def _pick_tile(n, pref):
    t = min(pref, n)
    while n % t:
        t //= 2
    return t


def kernel(x_prompt, x_sample, cache_ckv, cache_kpe, state_h, state_conv, page_table, meta_tokens,
           norm_mix, w_in, q_norm, w_uq, kv_norm, w_uk, w_uv, conv_w, conv_b, w_ra, b_ra, w_ri, b_ri,
           lru_a, w_out, norm_ffn, w_gate, w_up, w_down, final_norm):
    params = dict(norm_mix=norm_mix, w_in=w_in, q_norm=q_norm, w_uq=w_uq, kv_norm=kv_norm, w_uk=w_uk,
                  w_uv=w_uv, conv_w=conv_w, conv_b=conv_b, w_ra=w_ra, b_ra=b_ra, w_ri=w_ri, b_ri=b_ri,
                  lru_a=lru_a, w_out=w_out, norm_ffn=norm_ffn, w_gate=w_gate, w_up=w_up, w_down=w_down)
    depth = w_in.shape[0]
    B, S, _ = x_prompt.shape
    DB, T, _ = x_sample.shape
    assert T >= CONV_W - 1
    past_len = page_table.shape[1] * PAGE_SIZE
    fnorm = final_norm.reshape(1, -1).astype(F32)

    tab_meta = _rope_table(jnp.arange(META_ROWS, dtype=F32) - META_PAD)
    tab_main = _rope_table(N_META + jnp.arange(S, dtype=F32))
    tab_samp = _rope_table(jnp.repeat(past_len + jnp.arange(T, dtype=F32), DB))

    xm = jnp.concatenate([jnp.zeros((META_PAD, D_MODEL), F32), meta_tokens.astype(F32)], axis=0)[None]
    xp = x_prompt
    xs = jnp.transpose(x_sample, (1, 0, 2)).reshape(T * DB, D_MODEL)
    zero_h = jnp.zeros((1, 1, D_RNN), F32)
    zero_c = jnp.zeros((1, SUBLANES, D_RNN), F32)
    tm_in = _pick_tile(S, 512)
    tm_out = _pick_tile(B * S, 512)
    tm_s = _pick_tile(T * DB, 512)

    outs = {k: [] for k in ("ckv_p", "kpe_p", "h_p", "cv_p", "ckv_s", "kpe_s", "h_s", "cv_s")}
    rope_lanes = slice(QK_NOPE, QK_NOPE + QK_ROPE)
    for l in range(depth):
        w = _prep_layer(l, params)
        last = l == depth - 1

        ckv_m, kpe_m, q_m, k_m, v_m, rnn_m, h_m, c_m = _in_prompt(
            xm, tab_meta, zero_h, zero_c, w, tm=META_ROWS, n_pad=META_PAD)
        attn_m = _attn_meta(q_m[0], k_m[0], v_m[0], n_pad=META_PAD)
        xm = _out_proj(xm[0], attn_m, rnn_m[0], w, fnorm, tm=META_ROWS, final=False)[None]

        ckv, kpe, q, k, v, rnn, h_f, c_f = _in_prompt(xp, tab_main, h_m, c_m, w, tm=tm_in, n_pad=0)
        attn = _attn_main(q, k, v, k_m[0], v_m[0], n_pad=META_PAD)
        xp = _out_proj(xp.reshape(B * S, D_MODEL), attn.reshape(B * S, -1), rnn.reshape(B * S, -1),
                       w, fnorm, tm=tm_out, final=last).reshape(B, S, D_MODEL)
        bc = lambda a: jnp.broadcast_to(a[None], (B,) + a.shape)
        outs["ckv_p"].append(jnp.concatenate([bc(ckv_m[0, META_PAD:]), ckv], axis=1))
        outs["kpe_p"].append(jnp.concatenate(
            [bc(kpe_m[0, META_PAD:, rope_lanes]), kpe[..., rope_lanes]], axis=1))
        outs["h_p"].append(h_f[:, 0])
        outs["cv_p"].append(c_f[:, SUBLANES - (CONV_W - 1):])

        cp = jnp.transpose(state_conv[l], (1, 0, 2))
        ckv_s, kpe_s, qlat, qrot, rnn_s, h_s, cv_s = _in_sample(
            xs, tab_samp, state_h[l], cp, w, nb=DB, nt=T)
        ckv_new = jnp.transpose(ckv_s.reshape(T, DB, KV_LORA), (1, 0, 2))
        kpe_new = jnp.transpose(kpe_s[:, rope_lanes].reshape(T, DB, QK_ROPE), (1, 0, 2))
        ql = jnp.transpose(qlat.reshape(N_HEADS, T, DB, KV_LORA), (2, 0, 1, 3)).reshape(DB, N_HEADS * T, KV_LORA)
        qp = qrot.reshape(T, DB, N_HEADS, LANES)[..., rope_lanes]
        qp = jnp.transpose(qp, (1, 2, 0, 3)).reshape(DB, N_HEADS * T, QK_ROPE)
        o_lat = _decode_attn(page_table, ql, qp, ckv_new, kpe_new, cache_ckv, cache_kpe, layer=l)
        ol = jnp.transpose(o_lat.reshape(DB, N_HEADS, T, KV_LORA), (1, 2, 0, 3))
        ol = ol.reshape(N_HEADS, T * DB, KV_LORA).astype(BF16)
        xs = _out_proj(xs, ol, rnn_s, w, fnorm, tm=tm_s, final=last, latent=True)
        outs["ckv_s"].append(ckv_new)
        outs["kpe_s"].append(kpe_new)
        outs["h_s"].append(h_s)
        outs["cv_s"].append(jnp.transpose(cv_s, (1, 0, 2)))

    y_sample = jnp.transpose(xs.reshape(T, DB, D_MODEL), (1, 0, 2))
    st = lambda k: jnp.stack(outs[k])
    return (xp, y_sample, st("ckv_p"), st("kpe_p"), st("h_p"), st("cv_p"),
            st("ckv_s"), st("kpe_s"), st("h_s"), st("cv_s"))
```

```python
import functools
import math

import jax
import jax.numpy as jnp
from jax import lax
from jax.experimental import pallas as pl
from jax.experimental.pallas import tpu as pltpu

F32 = jnp.float32
BF16 = jnp.bfloat16

D_MODEL = 1024
N_META = 16
N_HEADS = 8
QK_NOPE = 64
QK_ROPE = 32
V_DIM = 64
Q_LORA = 384
KV_LORA = 256
ROPE_THETA = 10000.0
D_RNN = 512
N_RNN_BLOCKS = 8
RNN_BLOCK = D_RNN // N_RNN_BLOCKS
CONV_W = 4
LRU_C = 8.0
D_FF = 2816
RMS_EPS = 1e-6
PAGE_SIZE = 128
ATTN_SCALE = 1.0 / math.sqrt(QK_NOPE + QK_ROPE)

LANES = 128
SUBLANES = 8
VMEM_LIMIT = 56 * 1024 * 1024

OFF_CQ = 0
OFF_CKV = OFF_CQ + Q_LORA
OFF_U = OFF_CKV + KV_LORA
OFF_G = OFF_U + D_RNN
OFF_KPA = OFF_G + D_RNN
OFF_KPB = OFF_KPA + LANES
IN_COLS = OFF_KPB + LANES
QAB_COLS = 2 * N_HEADS * LANES
KV_COLS = N_HEADS * LANES + N_HEADS * V_DIM

META_ROWS = 128
META_PAD = META_ROWS - N_META
ATTN_TQ = 256
FF_CHUNK = 256
DEC_PAGES = 32
DEC_SPLIT = 4


def _rms(x, g):
    return x * lax.rsqrt(jnp.mean(x * x, axis=-1, keepdims=True) + RMS_EPS) * g


def _dot(a, b):
    return jnp.dot(a, b, preferred_element_type=F32)


def _dot_nt(a, b):
    return lax.dot_general(a, b, (((1,), (1,)), ((), ())), preferred_element_type=F32)


def _softplus(z):
    return jnp.maximum(z, 0.0) + jnp.log1p(jnp.exp(-jnp.abs(z)))


def _gelu_tanh(x):
    cdf = 0.5 * (1.0 + jnp.tanh(math.sqrt(2.0 / math.pi) * (x + 0.044715 * (x * x * x))))
    return x * cdf


def _const_spec(shape):
    zeros = (0,) * len(shape)
    return pl.BlockSpec(shape, lambda *_: zeros, pipeline_mode=pl.Buffered(1))


def _front(x_ref, tab_ref, nm_ref, win_ref, qn_ref, wuq_ref, kvn_ref, proj_s, qab_s):
    xn = _rms(x_ref[...], nm_ref[...]).astype(BF16)
    proj_s[...] = _dot(xn, win_ref[...])
    cqn = _rms(proj_s[:, OFF_CQ:OFF_CQ + Q_LORA], qn_ref[...]).astype(BF16)
    qab_s[...] = _dot(cqn, wuq_ref[...])
    ckv_n = _rms(proj_s[:, OFF_CKV:OFF_CKV + KV_LORA], kvn_ref[...])
    ck_t = tab_ref[:, 2 * LANES:3 * LANES]
    sk_t = tab_ref[:, 3 * LANES:4 * LANES]
    kpe = proj_s[:, OFF_KPA:OFF_KPA + LANES] * ck_t + proj_s[:, OFF_KPB:OFF_KPB + LANES] * sk_t
    return ckv_n, kpe


def _q_head(qab_s, tab_ref, h):
    cq_t = tab_ref[:, 0:LANES]
    sq_t = tab_ref[:, LANES:2 * LANES]
    lo = h * LANES
    hi = N_HEADS * LANES + h * LANES
    return qab_s[:, lo:lo + LANES] * cq_t + qab_s[:, hi:hi + LANES] * sq_t


def _lru_terms(xc, wg_ref, bg_ref, la_ref):
    gates = _dot(xc.astype(BF16), wg_ref[...]) + bg_ref[...]
    r = jax.nn.sigmoid(gates[:, 0:D_RNN])
    i = jax.nn.sigmoid(gates[:, D_RNN:2 * D_RNN])
    log_a = (-LRU_C * r) * _softplus(-la_ref[...])
    a = jnp.exp(log_a)
    mult = jnp.sqrt(-jnp.tanh(log_a) * (a * a + 1.0))
    return a, mult * (i * xc)


def _in_prompt_kernel(x_ref, tab_ref, h0_ref, c0_ref,
                      nm_ref, win_ref, qn_ref, wuq_ref, kvn_ref, wkv_ref,
                      cw_ref, cb_ref, wg_ref, bg_ref, la_ref,
                      ckv_ref, kpe_ref, q_ref, k_ref, v_ref, rnn_ref, hf_ref, cf_ref,
                      proj_s, qab_s, kv_s, ubuf, a_s, b_s, h_s, hc_s, *, tm, n_pad):
    t = pl.program_id(1)
    ckv_n, kpe = _front(x_ref, tab_ref, nm_ref, win_ref, qn_ref, wuq_ref, kvn_ref, proj_s, qab_s)
    ckv_ref[...] = ckv_n
    kpe_ref[...] = kpe
    kv_s[...] = _dot(ckv_n.astype(BF16), wkv_ref[...])
    for h in range(N_HEADS):
        sl = slice(h * LANES, (h + 1) * LANES)
        q_ref[:, sl] = _q_head(qab_s, tab_ref, h).astype(BF16)
        k_ref[:, sl] = (kv_s[:, sl] + kpe).astype(BF16)
    v_ref[...] = kv_s[:, N_HEADS * LANES:KV_COLS].astype(BF16)

    @pl.when(t == 0)
    def _():
        ubuf[0:SUBLANES, :] = c0_ref[...]
        hc_s[...] = h0_ref[...]

    @pl.when(t > 0)
    def _():
        ubuf[0:SUBLANES, :] = ubuf[tm:tm + SUBLANES, :]

    u = proj_s[:, OFF_U:OFF_U + D_RNN]
    ubuf[SUBLANES:SUBLANES + tm, :] = u
    cw = cw_ref[...]
    xc = cb_ref[...]
    for k in range(CONV_W - 1):
        lo = SUBLANES - (CONV_W - 1) + k
        xc = xc + ubuf[lo:lo + tm, :] * cw[k:k + 1, :]
    xc = xc + u * cw[CONV_W - 1:CONV_W, :]
    a, b = _lru_terms(xc, wg_ref, bg_ref, la_ref)
    row = lax.broadcasted_iota(jnp.int32, (tm, D_RNN), 0)
    if n_pad:
        b = jnp.where(row + t * tm >= n_pad, b, 0.0)

    sub = row & (SUBLANES - 1)
    for d in (1, 2, 4):
        keep = sub >= d
        b = jnp.where(keep, a * pltpu.roll(b, d, 0) + b, b)
        a = jnp.where(keep, a * pltpu.roll(a, d, 0), a)
    a_s[...] = a
    b_s[...] = b

    def group(gi, h_in):
        o = pl.multiple_of(gi * SUBLANES, SUBLANES)
        hr = a_s[pl.ds(o, SUBLANES), :] * h_in + b_s[pl.ds(o, SUBLANES), :]
        h_s[pl.ds(o, SUBLANES), :] = hr
        return hr[SUBLANES - 1:SUBLANES, :]

    h_last = lax.fori_loop(0, tm // SUBLANES, group, hc_s[...], unroll=8)
    hc_s[...] = h_last
    hf_ref[...] = h_last
    cf_ref[...] = ubuf[tm:tm + SUBLANES, :]
    rnn_ref[...] = (h_s[...] * _gelu_tanh(proj_s[:, OFF_G:OFF_G + D_RNN])).astype(BF16)


def _in_prompt(x, tab, h0, c0, w, *, tm, n_pad):
    nb, L, _ = x.shape
    nt = L // tm
    assert nt * tm == L
    bstate = (lambda b, t: (b, 0, 0)) if h0.shape[0] == nb else (lambda b, t: (0, 0, 0))
    row_spec = lambda c: pl.BlockSpec((None, tm, c), lambda b, t: (b, t, 0))
    in_specs = [
        row_spec(D_MODEL),
        pl.BlockSpec((tm, 4 * LANES), lambda b, t: (t, 0)),
        pl.BlockSpec((None, 1, D_RNN), bstate),
        pl.BlockSpec((None, SUBLANES, D_RNN), bstate),
        _const_spec((1, D_MODEL)), _const_spec((D_MODEL, IN_COLS)),
        _const_spec((1, Q_LORA)), _const_spec((Q_LORA, QAB_COLS)),
        _const_spec((1, KV_LORA)), _const_spec((KV_LORA, KV_COLS)),
        _const_spec((CONV_W, D_RNN)), _const_spec((1, D_RNN)),
        _const_spec((D_RNN, 2 * D_RNN)), _const_spec((1, 2 * D_RNN)), _const_spec((1, D_RNN)),
    ]
    out_shape = [
        jax.ShapeDtypeStruct((nb, L, KV_LORA), F32),
        jax.ShapeDtypeStruct((nb, L, LANES), F32),
        jax.ShapeDtypeStruct((nb, L, N_HEADS * LANES), BF16),
        jax.ShapeDtypeStruct((nb, L, N_HEADS * LANES), BF16),
        jax.ShapeDtypeStruct((nb, L, N_HEADS * V_DIM), BF16),
        jax.ShapeDtypeStruct((nb, L, D_RNN), BF16),
        jax.ShapeDtypeStruct((nb, 1, D_RNN), F32),
        jax.ShapeDtypeStruct((nb, SUBLANES, D_RNN), F32),
    ]
    out_specs = [
        row_spec(KV_LORA), row_spec(LANES), row_spec(N_HEADS * LANES), row_spec(N_HEADS * LANES),
        row_spec(N_HEADS * V_DIM), row_spec(D_RNN),
        pl.BlockSpec((None, 1, D_RNN), lambda b, t: (b, 0, 0)),
        pl.BlockSpec((None, SUBLANES, D_RNN), lambda b, t: (b, 0, 0)),
    ]
    scratch = [
        pltpu.VMEM((tm, IN_COLS), F32), pltpu.VMEM((tm, QAB_COLS), F32), pltpu.VMEM((tm, KV_COLS), F32),
        pltpu.VMEM((tm + SUBLANES, D_RNN), F32),
        pltpu.VMEM((tm, D_RNN), F32), pltpu.VMEM((tm, D_RNN), F32), pltpu.VMEM((tm, D_RNN), F32),
        pltpu.VMEM((1, D_RNN), F32),
    ]
    return pl.pallas_call(
        functools.partial(_in_prompt_kernel, tm=tm, n_pad=n_pad),
        grid=(nb, nt), in_specs=in_specs, out_specs=out_specs, out_shape=out_shape,
        scratch_shapes=scratch, name="in_prompt",
        compiler_params=pltpu.CompilerParams(
            dimension_semantics=("arbitrary", "arbitrary"), vmem_limit_bytes=VMEM_LIMIT),
    )(x, tab, h0, c0, w["norm_mix"], w["w_in"], w["q_norm"], w["w_uq"], w["kv_norm"], w["w_kv"],
      w["conv_w"], w["conv_b"], w["w_gates"], w["b_gates"], w["lru_a"])


def _in_sample_kernel(x_ref, tab_ref, h0_ref, cp_ref,
                      nm_ref, win_ref, qn_ref, wuq_ref, kvn_ref, wukt_ref,
                      cw_ref, cb_ref, wg_ref, bg_ref, la_ref,
                      ckv_ref, kpe_ref, qlat_ref, qrot_ref, rnn_ref, hf_ref, cf_ref,
                      proj_s, qab_s, xc_s, h_s, *, nb, nt):
    ckv_n, kpe = _front(x_ref, tab_ref, nm_ref, win_ref, qn_ref, wuq_ref, kvn_ref, proj_s, qab_s)
    ckv_ref[...] = ckv_n
    kpe_ref[...] = kpe
    for h in range(N_HEADS):
        qh = _q_head(qab_s, tab_ref, h).astype(BF16)
        qrot_ref[:, h * LANES:(h + 1) * LANES] = qh
        qlat_ref[h] = _dot(qh, wukt_ref[h]).astype(BF16)

    cw = cw_ref[...]

    def xpad(j):
        if j < CONV_W - 1:
            return cp_ref[j]
        jj = j - (CONV_W - 1)
        return proj_s[jj * nb:(jj + 1) * nb, OFF_U:OFF_U + D_RNN]

    for t in range(nt):
        xc = cb_ref[...]
        for k in range(CONV_W):
            xc = xc + xpad(t + k) * cw[k:k + 1, :]
        xc_s[t * nb:(t + 1) * nb, :] = xc
    a, b = _lru_terms(xc_s[...], wg_ref, bg_ref, la_ref)
    h = h0_ref[...]
    for t in range(nt):
        h = a[t * nb:(t + 1) * nb, :] * h + b[t * nb:(t + 1) * nb, :]
        h_s[t * nb:(t + 1) * nb, :] = h
    hf_ref[...] = h
    for j in range(CONV_W - 1):
        cf_ref[j] = xpad(nt + j)
    rnn_ref[...] = (h_s[...] * _gelu_tanh(proj_s[:, OFF_G:OFF_G + D_RNN])).astype(BF16)


def _in_sample(x, tab, h0, cp, w, *, nb, nt):
    n = nb * nt
    full = lambda shape: pl.BlockSpec(shape, lambda i: (0,) * len(shape))
    in_specs = [
        full((n, D_MODEL)), full((n, 4 * LANES)), full((nb, D_RNN)), full((CONV_W - 1, nb, D_RNN)),
        full((1, D_MODEL)), full((D_MODEL, IN_COLS)), full((1, Q_LORA)), full((Q_LORA, QAB_COLS)),
        full((1, KV_LORA)), full((N_HEADS, LANES, KV_LORA)),
        full((CONV_W, D_RNN)), full((1, D_RNN)), full((D_RNN, 2 * D_RNN)), full((1, 2 * D_RNN)),
        full((1, D_RNN)),
    ]
    out_shape = [
        jax.ShapeDtypeStruct((n, KV_LORA), F32),
        jax.ShapeDtypeStruct((n, LANES), F32),
        jax.ShapeDtypeStruct((N_HEADS, n, KV_LORA), BF16),
        jax.ShapeDtypeStruct((n, N_HEADS * LANES), BF16),
        jax.ShapeDtypeStruct((n, D_RNN), BF16),
        jax.ShapeDtypeStruct((nb, D_RNN), F32),
        jax.ShapeDtypeStruct((CONV_W - 1, nb, D_RNN), F32),
    ]
    out_specs = [full(s.shape) for s in out_shape]
    scratch = [
        pltpu.VMEM((n, IN_COLS), F32), pltpu.VMEM((n, QAB_COLS), F32),
        pltpu.VMEM((n, D_RNN), F32), pltpu.VMEM((n, D_RNN), F32),
    ]
    return pl.pallas_call(
        functools.partial(_in_sample_kernel, nb=nb, nt=nt),
        grid=(1,), in_specs=in_specs, out_specs=out_specs, out_shape=out_shape,
        scratch_shapes=scratch, name="in_sample",
        compiler_params=pltpu.CompilerParams(
            dimension_semantics=("arbitrary",), vmem_limit_bytes=VMEM_LIMIT),
    )(x, tab, h0, cp, w["norm_mix"], w["w_in"], w["q_norm"], w["w_uq"], w["kv_norm"], w["w_ukt"],
      w["conv_w"], w["conv_b"], w["w_gates"], w["b_gates"], w["lru_a"])


def _softmax_init(s, v):
    m = jnp.max(s, axis=-1, keepdims=True)
    p = jnp.exp(s - m)
    return m, jnp.sum(p, axis=-1, keepdims=True), _dot(p.astype(BF16), v)


def _merge_pair(st0, st1):
    o0 = st0[2] / st0[1]
    o1 = st1[2] / st1[1]
    lane = lax.broadcasted_iota(jnp.int32, o0.shape, 1)
    return jnp.where(lane < V_DIM, o0, o1).astype(BF16)


def _attn_meta_kernel(q_ref, k_ref, v_ref, o_ref, *, n_pad):
    n = q_ref.shape[0]
    row = lax.broadcasted_iota(jnp.int32, (n, n), 0)
    col = lax.broadcasted_iota(jnp.int32, (n, n), 1)
    mask = (col <= row) & ((col >= n_pad) | (row < n_pad))
    v = v_ref[...]
    sts = []
    for hh in range(2):
        sl = slice(hh * LANES, (hh + 1) * LANES)
        s = jnp.where(mask, _dot_nt(q_ref[:, sl], k_ref[:, sl]), -jnp.inf)
        sts.append(_softmax_init(s, v))
    o_ref[...] = _merge_pair(*sts)


def _attn_meta(q, k, v, *, n_pad):
    n = q.shape[0]
    return pl.pallas_call(
        functools.partial(_attn_meta_kernel, n_pad=n_pad),
        grid=(N_HEADS // 2,),
        in_specs=[pl.BlockSpec((n, 2 * LANES), lambda p: (0, p)),
                  pl.BlockSpec((n, 2 * LANES), lambda p: (0, p)),
                  pl.BlockSpec((n, LANES), lambda p: (0, p))],
        out_specs=pl.BlockSpec((n, LANES), lambda p: (0, p)),
        out_shape=jax.ShapeDtypeStruct((n, N_HEADS * V_DIM), BF16),
        name="attn_meta",
        compiler_params=pltpu.CompilerParams(dimension_semantics=("arbitrary",)),
    )(q, k, v)


def _attn_main_kernel(q_ref, k_ref, v_ref, kp_ref, vp_ref, o_ref, *, tq, n_pad):
    L = q_ref.shape[0]
    npre = kp_ref.shape[0]
    pre_ok = lax.broadcasted_iota(jnp.int32, (tq, npre), 1) >= n_pad
    causal = (lax.broadcasted_iota(jnp.int32, (tq, tq), 1)
              <= lax.broadcasted_iota(jnp.int32, (tq, tq), 0))
    rowmax = lambda s: jnp.max(s, axis=-1, keepdims=True)
    rowsum = lambda p: jnp.sum(p, axis=-1, keepdims=True)
    vp = vp_ref[...]
    for qb in range(L // tq):
        r0 = qb * tq
        n = r0 + tq
        sts = []
        for hh in range(2):
            hs = slice(hh * LANES, (hh + 1) * LANES)
            q = q_ref[r0:n, hs]
            sp = jnp.where(pre_ok, _dot_nt(q, kp_ref[:, hs]), -jnp.inf)
            sd = jnp.where(causal, _dot_nt(q, k_ref[r0:n, hs]), -jnp.inf)
            m = jnp.maximum(rowmax(sp), rowmax(sd))
            if qb:
                sm = _dot_nt(q, k_ref[0:r0, hs])
                m = jnp.maximum(m, rowmax(sm))
            pp = jnp.exp(sp - m)
            pd = jnp.exp(sd - m)
            l = rowsum(pp) + rowsum(pd)
            acc = _dot(pp.astype(BF16), vp) + _dot(pd.astype(BF16), v_ref[r0:n, :])
            if qb:
                pm = jnp.exp(sm - m)
                l = l + rowsum(pm)
                acc = acc + _dot(pm.astype(BF16), v_ref[0:r0, :])
            sts.append((m, l, acc))
        o_ref[r0:n, :] = _merge_pair(*sts)


def _attn_main(q, k, v, kpre, vpre, *, n_pad):
    B, L, _ = q.shape
    npre = kpre.shape[0]
    tq = min(ATTN_TQ, L)
    assert L % tq == 0
    return pl.pallas_call(
        functools.partial(_attn_main_kernel, tq=tq, n_pad=n_pad),
        grid=(B, N_HEADS // 2),
        in_specs=[pl.BlockSpec((None, L, 2 * LANES), lambda b, p: (b, 0, p)),
                  pl.BlockSpec((None, L, 2 * LANES), lambda b, p: (b, 0, p)),
                  pl.BlockSpec((None, L, LANES), lambda b, p: (b, 0, p)),
                  pl.BlockSpec((npre, 2 * LANES), lambda b, p: (0, p)),
                  pl.BlockSpec((npre, LANES), lambda b, p: (0, p))],
        out_specs=pl.BlockSpec((None, L, LANES), lambda b, p: (b, 0, p)),
        out_shape=jax.ShapeDtypeStruct((B, L, N_HEADS * V_DIM), BF16),
        name="attn_main",
        compiler_params=pltpu.CompilerParams(
            dimension_semantics=("arbitrary", "arbitrary"), vmem_limit_bytes=VMEM_LIMIT),
    )(q, k, v, kpre, vpre)


def _decode_kernel(pt_ref, ql_ref, qp_ref, cn_ref, kn_ref, ckv_hbm, kpe_hbm, o_ref,
                   cbuf, kbuf, ncbuf, nkbuf, sem, m_s, l_s, acc_s, *, layer, G, NC, T):
    b = pl.program_id(0)
    c = pl.program_id(1)
    n = b * NC + c
    total = pl.num_programs(0) * NC
    slot = lax.rem(n, 2)

    def copies(bb, cc, sl):
        out = []
        for g in range(G):
            page = pt_ref[bb, cc * G + g]
            out.append(pltpu.make_async_copy(ckv_hbm.at[layer, page], cbuf.at[sl, g], sem.at[0, sl]))
            out.append(pltpu.make_async_copy(kpe_hbm.at[layer, page], kbuf.at[sl, g], sem.at[1, sl]))
        return out

    @pl.when(n == 0)
    def _():
        ncbuf[...] = jnp.zeros_like(ncbuf)
        nkbuf[...] = jnp.zeros_like(nkbuf)
        for cp in copies(b, c, slot):
            cp.start()

    @pl.when(n + 1 < total)
    def _():
        n1 = n + 1
        for cp in copies(n1 // NC, lax.rem(n1, NC), 1 - slot):
            cp.start()

    pltpu.make_async_copy(ckv_hbm.at[layer, pl.ds(0, G)], cbuf.at[slot], sem.at[0, slot]).wait()
    pltpu.make_async_copy(kpe_hbm.at[layer, pl.ds(0, G)], kbuf.at[slot], sem.at[1, slot]).wait()

    ql = ql_ref[...].astype(F32)
    qp = qp_ref[...].astype(F32)

    @pl.when(c == 0)
    def _():
        m_s[...] = jnp.full_like(m_s, -jnp.inf)
        l_s[...] = jnp.zeros_like(l_s)
        acc_s[...] = jnp.zeros_like(acc_s)

    def partial_softmax(kv, kpt, mask_fn=None):
        s = _dot_nt(ql, kv) + _dot(qp, kpt)
        if mask_fn is not None:
            s = mask_fn(s)
        m = jnp.max(s, axis=-1, keepdims=True)
        p = jnp.exp(s - m)
        return m, jnp.sum(p, axis=-1, keepdims=True), _dot(p, kv)

    def merge(parts):
        m_old = m_s[...]
        m_new = m_old
        for m, _, _ in parts:
            m_new = jnp.maximum(m_new, m)
        w_old = jnp.exp(m_old - m_new)
        l = w_old * l_s[...]
        acc = w_old * acc_s[...]
        for m, li, ai in parts:
            wi = jnp.exp(m - m_new)
            l = l + wi * li
            acc = acc + wi * ai
        return m_new, l, acc

    pp = G // DEC_SPLIT
    parts = []
    for sc in range(DEC_SPLIT):
        kv = cbuf[slot, sc * pp:(sc + 1) * pp].reshape(pp * PAGE_SIZE, KV_LORA)
        kpt = jnp.concatenate([kbuf[slot, sc * pp + g] for g in range(pp)], axis=1)
        parts.append(partial_softmax(kv, kpt))
    m_s[...], l_s[...], acc_s[...] = merge(parts)

    @pl.when(c == NC - 1)
    def _():
        ncbuf[0:T, :] = cn_ref[...]
        nkbuf[:, 0:T] = kn_ref[...]

        def mask(s):
            tq = lax.rem(lax.broadcasted_iota(jnp.int32, s.shape, 0), T)
            col = lax.broadcasted_iota(jnp.int32, s.shape, 1)
            return jnp.where(col <= tq, s, -jnp.inf)

        m, l, acc = merge([partial_softmax(ncbuf[...], nkbuf[...], mask)])
        o_ref[...] = acc / l


def _decode_attn(page_table, qlat, qpe, ckv_new, kpe_new_t, cache_ckv, cache_kpe_t, *, layer):
    DB, rows, _ = qlat.shape
    T = ckv_new.shape[1]
    n_pages = page_table.shape[1]
    G = min(DEC_PAGES, n_pages)
    assert n_pages % G == 0 and G % DEC_SPLIT == 0
    NC = n_pages // G
    grid_spec = pltpu.PrefetchScalarGridSpec(
        num_scalar_prefetch=1,
        grid=(DB, NC),
        in_specs=[
            pl.BlockSpec((None, rows, KV_LORA), lambda b, c, pt: (b, 0, 0)),
            pl.BlockSpec((None, rows, QK_ROPE), lambda b, c, pt: (b, 0, 0)),
            pl.BlockSpec((None, T, KV_LORA), lambda b, c, pt: (b, 0, 0)),
            pl.BlockSpec((None, QK_ROPE, T), lambda b, c, pt: (b, 0, 0)),
            pl.BlockSpec(memory_space=pl.ANY),
            pl.BlockSpec(memory_space=pl.ANY),
        ],
        out_specs=pl.BlockSpec((None, rows, KV_LORA), lambda b, c, pt: (b, 0, 0)),
        scratch_shapes=[
            pltpu.VMEM((2, G, PAGE_SIZE, KV_LORA), F32),
            pltpu.VMEM((2, G, QK_ROPE, PAGE_SIZE), F32),
            pltpu.VMEM((PAGE_SIZE, KV_LORA), F32),
            pltpu.VMEM((QK_ROPE, PAGE_SIZE), F32),
            pltpu.SemaphoreType.DMA((2, 2)),
            pltpu.VMEM((rows, 1), F32), pltpu.VMEM((rows, 1), F32), pltpu.VMEM((rows, KV_LORA), F32),
        ],
    )
    return pl.pallas_call(
        functools.partial(_decode_kernel, layer=layer, G=G, NC=NC, T=T),
        grid_spec=grid_spec,
        out_shape=jax.ShapeDtypeStruct((DB, rows, KV_LORA), F32),
        name="decode_attn",
        compiler_params=pltpu.CompilerParams(
            dimension_semantics=("arbitrary", "arbitrary"), vmem_limit_bytes=VMEM_LIMIT),
    )(page_table, qlat, qpe, ckv_new, kpe_new_t, cache_ckv, cache_kpe_t)


def _ffn_tail(x1, nf_ref, wg_ref, wu_ref, wd_ref, fn_ref, o_ref, acc_s, final):
    xn = _rms(x1, nf_ref[...]).astype(BF16)
    for ci in range(D_FF // FF_CHUNK):
        sl = slice(ci * FF_CHUNK, (ci + 1) * FF_CHUNK)
        hg = _dot(xn, wg_ref[:, sl])
        hu = _dot(xn, wu_ref[:, sl])
        act = ((hg * jax.nn.sigmoid(hg)) * hu).astype(BF16)
        part = _dot(act, wd_ref[sl, :])
        if ci == 0:
            acc_s[...] = part
        else:
            acc_s[...] += part
    x2 = x1 + acc_s[...]
    if final:
        x2 = _rms(x2, fn_ref[...])
    o_ref[...] = x2


def _out_kernel(x_ref, a_ref, r_ref, wo_ref, nf_ref, wg_ref, wu_ref, wd_ref, fn_ref, o_ref, acc_s,
                *, final):
    na = a_ref.shape[1]
    y = _dot(a_ref[...], wo_ref[0:na, :]) + _dot(r_ref[...], wo_ref[na:, :])
    _ffn_tail(x_ref[...] + y, nf_ref, wg_ref, wu_ref, wd_ref, fn_ref, o_ref, acc_s, final)


def _out_latent_kernel(x_ref, ol_ref, wuv_ref, r_ref, wo_ref, nf_ref, wg_ref, wu_ref, wd_ref, fn_ref,
                       o_ref, acc_s, *, final):
    na = N_HEADS * V_DIM
    y = _dot(r_ref[...], wo_ref[na:, :])
    for h in range(N_HEADS):
        ah = _dot(ol_ref[h], wuv_ref[h]).astype(BF16)
        y = y + _dot(ah, wo_ref[h * V_DIM:(h + 1) * V_DIM, :])
    _ffn_tail(x_ref[...] + y, nf_ref, wg_ref, wu_ref, wd_ref, fn_ref, o_ref, acc_s, final)


def _out_proj(x, attn, rnn, w, final_norm, *, tm, final, latent=False):
    n = x.shape[0]
    assert n % tm == 0
    row = lambda c: pl.BlockSpec((tm, c), lambda i: (i, 0))
    wspecs = [_const_spec((D_MODEL, D_MODEL)), _const_spec((1, D_MODEL)),
              _const_spec((D_MODEL, D_FF)), _const_spec((D_MODEL, D_FF)), _const_spec((D_FF, D_MODEL)),
              _const_spec((1, D_MODEL))]
    wargs = (w["w_out"], w["norm_ffn"], w["w_gate"], w["w_up"], w["w_down"], final_norm)
    if latent:
        kern = functools.partial(_out_latent_kernel, final=final)
        in_specs = [row(D_MODEL), pl.BlockSpec((N_HEADS, tm, KV_LORA), lambda i: (0, i, 0)),
                    _const_spec((N_HEADS, KV_LORA, V_DIM)), row(D_RNN)] + wspecs
        args = (x, attn, w["w_uv3"], rnn) + wargs
    else:
        kern = functools.partial(_out_kernel, final=final)
        in_specs = [row(D_MODEL), row(N_HEADS * V_DIM), row(D_RNN)] + wspecs
        args = (x, attn, rnn) + wargs
    return pl.pallas_call(
        kern, grid=(n // tm,), in_specs=in_specs, out_specs=row(D_MODEL),
        out_shape=jax.ShapeDtypeStruct((n, D_MODEL), F32),
        scratch_shapes=[pltpu.VMEM((tm, D_MODEL), F32)],
        name="out_latent" if latent else "out_proj",
        compiler_params=pltpu.CompilerParams(
            dimension_semantics=("arbitrary",), vmem_limit_bytes=VMEM_LIMIT),
    )(*args)


def _rot_half_cols(w):
    half = QK_ROPE // 2
    return jnp.concatenate([-w[..., half:], w[..., :half]], axis=-1)


def _prep_layer(l, p):
    w_in = p["w_in"][l]
    d = w_in.shape[0]
    o_kpe = Q_LORA + KV_LORA
    kpe = w_in[:, o_kpe:o_kpe + QK_ROPE]
    z = lambda c: jnp.zeros((d, c), F32)
    w_in_r = jnp.concatenate(
        [w_in[:, :o_kpe], w_in[:, o_kpe + QK_ROPE:],
         z(QK_NOPE), kpe, z(LANES - QK_NOPE - QK_ROPE),
         z(QK_NOPE), _rot_half_cols(kpe), z(LANES - QK_NOPE - QK_ROPE)], axis=1)

    w_uq = p["w_uq"][l].reshape(Q_LORA, N_HEADS, QK_NOPE + QK_ROPE)
    nope, pe = w_uq[..., :QK_NOPE], w_uq[..., QK_NOPE:]
    zq = lambda c: jnp.zeros((Q_LORA, N_HEADS, c), F32)
    qa = jnp.concatenate([nope, pe, zq(LANES - QK_NOPE - QK_ROPE)], axis=-1)
    qb = jnp.concatenate([zq(QK_NOPE), _rot_half_cols(pe), zq(LANES - QK_NOPE - QK_ROPE)], axis=-1)
    w_uq_r = jnp.concatenate([qa.reshape(Q_LORA, -1), qb.reshape(Q_LORA, -1)], axis=1)

    w_uk = p["w_uk"][l]
    w_uv = p["w_uv"][l]
    w_k = jnp.concatenate([w_uk, jnp.zeros((KV_LORA, N_HEADS, LANES - QK_NOPE), F32)], axis=-1)
    w_kv = jnp.concatenate([w_k.reshape(KV_LORA, -1), w_uv.reshape(KV_LORA, -1)], axis=1)
    w_ukt = jnp.concatenate(
        [jnp.transpose(w_uk, (1, 2, 0)), jnp.zeros((N_HEADS, LANES - QK_NOPE, KV_LORA), F32)], axis=1)

    eye = jnp.eye(N_RNN_BLOCKS, dtype=F32)
    bd = lambda wb: jnp.einsum("nde,nm->ndme", wb, eye).reshape(D_RNN, D_RNN)
    w_gates = jnp.concatenate([bd(p["w_ra"][l]), bd(p["w_ri"][l])], axis=1)
    row = lambda v: v.reshape(1, -1).astype(F32)
    return {
        "norm_mix": row(p["norm_mix"][l]), "w_in": w_in_r.astype(BF16),
        "q_norm": row(p["q_norm"][l]), "w_uq": w_uq_r.astype(BF16),
        "kv_norm": row(p["kv_norm"][l]), "w_kv": w_kv.astype(BF16), "w_ukt": w_ukt.astype(BF16),
        "w_uv3": jnp.transpose(w_uv, (1, 0, 2)).astype(BF16),
        "conv_w": p["conv_w"][l].astype(F32), "conv_b": row(p["conv_b"][l]),
        "w_gates": w_gates.astype(BF16),
        "b_gates": jnp.concatenate([p["b_ra"][l], p["b_ri"][l]]).reshape(1, -1).astype(F32),
        "lru_a": row(p["lru_a"][l]),
        "w_out": p["w_out"][l].astype(BF16), "norm_ffn": row(p["norm_ffn"][l]),
        "w_gate": p["w_gate"][l].astype(BF16), "w_up": p["w_up"][l].astype(BF16),
        "w_down": p["w_down"][l].astype(BF16),
    }


def _rope_table(pos):
    half = QK_ROPE // 2
    inv = ROPE_THETA ** (-jnp.arange(half, dtype=F32) / half)
    ang = pos[:, None] * inv[None, :]
    cos = jnp.tile(jnp.cos(ang), (1, 2))
    sin = jnp.tile(jnp.sin(ang), (1, 2))
    n = pos.shape[0]
    one = jnp.ones((n, QK_NOPE), F32)
    z_lo = jnp.zeros((n, QK_NOPE), F32)
    z_hi = jnp.zeros((n, LANES - QK_NOPE - QK_ROPE), F32)
    return jnp.concatenate(
        [one * ATTN_SCALE, cos * ATTN_SCALE, z_hi, z_lo, sin * ATTN_SCALE, z_hi,
         z_lo, cos, z_hi, z_lo, sin, z_hi], axis=1)


def _pick_tile(n, pref):
    t = min(pref, n)
    while n % t:
        t //= 2
    return t


def kernel(x_prompt, x_sample, cache_ckv, cache_kpe, state_h, state_conv, page_table, meta_tokens,
           norm_mix, w_in, q_norm, w_uq, kv_norm, w_uk, w_uv, conv_w, conv_b, w_ra, b_ra, w_ri, b_ri,
           lru_a, w_out, norm_ffn, w_gate, w_up, w_down, final_norm):
    params = dict(norm_mix=norm_mix, w_in=w_in, q_norm=q_norm, w_uq=w_uq, kv_norm=kv_norm, w_uk=w_uk,
                  w_uv=w_uv, conv_w=conv_w, conv_b=conv_b, w_ra=w_ra, b_ra=b_ra, w_ri=w_ri, b_ri=b_ri,
                  lru_a=lru_a, w_out=w_out, norm_ffn=norm_ffn, w_gate=w_gate, w_up=w_up, w_down=w_down)
    depth = w_in.shape[0]
    B, S, _ = x_prompt.shape
    DB, T, _ = x_sample.shape
    assert T >= CONV_W - 1
    past_len = page_table.shape[1] * PAGE_SIZE
    fnorm = final_norm.reshape(1, -1).astype(F32)

    tab_meta = _rope_table(jnp.arange(META_ROWS, dtype=F32) - META_PAD)
    tab_main = _rope_table(N_META + jnp.arange(S, dtype=F32))
    tab_samp = _rope_table(jnp.repeat(past_len + jnp.arange(T, dtype=F32), DB))

    xm = jnp.concatenate([jnp.zeros((META_PAD, D_MODEL), F32), meta_tokens.astype(F32)], axis=0)[None]
    xp = x_prompt
    xs = jnp.transpose(x_sample, (1, 0, 2)).reshape(T * DB, D_MODEL)
    zero_h = jnp.zeros((1, 1, D_RNN), F32)
    zero_c = jnp.zeros((1, SUBLANES, D_RNN), F32)
    tm_in = _pick_tile(S, 512)
    tm_out = _pick_tile(B * S, 512)
    tm_s = _pick_tile(T * DB, 512)

    cache_kpe_t = jnp.transpose(cache_kpe, (0, 1, 3, 2))
    outs = {k: [] for k in ("ckv_p", "kpe_p", "h_p", "cv_p", "ckv_s", "kpe_s", "h_s", "cv_s")}
    rope_lanes = slice(QK_NOPE, QK_NOPE + QK_ROPE)
    for l in range(depth):
        w = _prep_layer(l, params)
        last = l == depth - 1

        ckv_m, kpe_m, q_m, k_m, v_m, rnn_m, h_m, c_m = _in_prompt(
            xm, tab_meta, zero_h, zero_c, w, tm=META_ROWS, n_pad=META_PAD)
        attn_m = _attn_meta(q_m[0], k_m[0], v_m[0], n_pad=META_PAD)
        xm = _out_proj(xm[0], attn_m, rnn_m[0], w, fnorm, tm=META_ROWS, final=False)[None]

        ckv, kpe, q, k, v, rnn, h_f, c_f = _in_prompt(xp, tab_main, h_m, c_m, w, tm=tm_in, n_pad=0)
        attn = _attn_main(q, k, v, k_m[0], v_m[0], n_pad=META_PAD)
        xp = _out_proj(xp.reshape(B * S, D_MODEL), attn.reshape(B * S, -1), rnn.reshape(B * S, -1),
                       w, fnorm, tm=tm_out, final=last).reshape(B, S, D_MODEL)
        bc = lambda a: jnp.broadcast_to(a[None], (B,) + a.shape)
        outs["ckv_p"].append(jnp.concatenate([bc(ckv_m[0, META_PAD:]), ckv], axis=1))
        outs["kpe_p"].append(jnp.concatenate(
            [bc(kpe_m[0, META_PAD:, rope_lanes]), kpe[..., rope_lanes]], axis=1))
        outs["h_p"].append(h_f[:, 0])
        outs["cv_p"].append(c_f[:, SUBLANES - (CONV_W - 1):])

        cp = jnp.transpose(state_conv[l], (1, 0, 2))
        ckv_s, kpe_s, qlat, qrot, rnn_s, h_s, cv_s = _in_sample(
            xs, tab_samp, state_h[l], cp, w, nb=DB, nt=T)
        ckv_new = jnp.transpose(ckv_s.reshape(T, DB, KV_LORA), (1, 0, 2))
        kpe_new = jnp.transpose(kpe_s[:, rope_lanes].reshape(T, DB, QK_ROPE), (1, 0, 2))
        ql = jnp.transpose(qlat.reshape(N_HEADS, T, DB, KV_LORA), (2, 0, 1, 3)).reshape(DB, N_HEADS * T, KV_LORA)
        qp = qrot.reshape(T, DB, N_HEADS, LANES)[..., rope_lanes]
        qp = jnp.transpose(qp, (1, 2, 0, 3)).reshape(DB, N_HEADS * T, QK_ROPE)
        o_lat = _decode_attn(page_table, ql, qp, ckv_new, jnp.transpose(kpe_new, (0, 2, 1)),
                             cache_ckv, cache_kpe_t, layer=l)
        ol = jnp.transpose(o_lat.reshape(DB, N_HEADS, T, KV_LORA), (1, 2, 0, 3))
        ol = ol.reshape(N_HEADS, T * DB, KV_LORA).astype(BF16)
        xs = _out_proj(xs, ol, rnn_s, w, fnorm, tm=tm_s, final=last, latent=True)
        outs["ckv_s"].append(ckv_new)
        outs["kpe_s"].append(kpe_new)
        outs["h_s"].append(h_s)
        outs["cv_s"].append(jnp.transpose(cv_s, (1, 0, 2)))

    y_sample = jnp.transpose(xs.reshape(T, DB, D_MODEL), (1, 0, 2))
    st = lambda k: jnp.stack(outs[k])
    return (xp, y_sample, st("ckv_p"), st("kpe_p"), st("h_p"), st("cv_p"),
            st("ckv_s"), st("kpe_s"), st("h_s"), st("cv_s"))
```

```python
import functools
import math

import jax
import jax.numpy as jnp
from jax import lax
from jax.experimental import pallas as pl
from jax.experimental.pallas import tpu as pltpu

F32 = jnp.float32
BF16 = jnp.bfloat16

D_MODEL = 1024
N_META = 16
N_HEADS = 8
QK_NOPE = 64
QK_ROPE = 32
V_DIM = 64
Q_LORA = 384
KV_LORA = 256
ROPE_THETA = 10000.0
D_RNN = 512
N_RNN_BLOCKS = 8
RNN_BLOCK = D_RNN // N_RNN_BLOCKS
CONV_W = 4
LRU_C = 8.0
D_FF = 2816
RMS_EPS = 1e-6
PAGE_SIZE = 128
ATTN_SCALE = 1.0 / math.sqrt(QK_NOPE + QK_ROPE)

LANES = 128
SUBLANES = 8
VMEM_LIMIT = 56 * 1024 * 1024

OFF_CQ = 0
OFF_CKV = OFF_CQ + Q_LORA
OFF_U = OFF_CKV + KV_LORA
OFF_G = OFF_U + D_RNN
OFF_KPA = OFF_G + D_RNN
OFF_KPB = OFF_KPA + LANES
IN_COLS = OFF_KPB + LANES
QAB_COLS = 2 * N_HEADS * LANES
KV_COLS = N_HEADS * LANES + N_HEADS * V_DIM

META_ROWS = 128
META_PAD = META_ROWS - N_META
ATTN_TQ = 256
ATTN_QGROUP = 1
FF_CHUNK = 256
DEC_PAGES = 32
DEC_SPLIT = 4


def _rms(x, g):
    return x * lax.rsqrt(jnp.mean(x * x, axis=-1, keepdims=True) + RMS_EPS) * g


def _dot(a, b):
    return jnp.dot(a, b, preferred_element_type=F32)


def _dot_nt(a, b):
    return lax.dot_general(a, b, (((1,), (1,)), ((), ())), preferred_element_type=F32)


def _softplus(z):
    return jnp.maximum(z, 0.0) + jnp.log1p(jnp.exp(-jnp.abs(z)))


def _gelu_tanh(x):
    cdf = 0.5 * (1.0 + jnp.tanh(math.sqrt(2.0 / math.pi) * (x + 0.044715 * (x * x * x))))
    return x * cdf


def _const_spec(shape):
    zeros = (0,) * len(shape)
    return pl.BlockSpec(shape, lambda *_: zeros, pipeline_mode=pl.Buffered(1))


def _front(x_ref, tab_ref, nm_ref, win_ref, qn_ref, wuq_ref, kvn_ref, proj_s, qab_s):
    xn = _rms(x_ref[...], nm_ref[...]).astype(BF16)
    proj_s[...] = _dot(xn, win_ref[...])
    cqn = _rms(proj_s[:, OFF_CQ:OFF_CQ + Q_LORA], qn_ref[...]).astype(BF16)
    qab_s[...] = _dot(cqn, wuq_ref[...])
    ckv_n = _rms(proj_s[:, OFF_CKV:OFF_CKV + KV_LORA], kvn_ref[...])
    ck_t = tab_ref[:, 2 * LANES:3 * LANES]
    sk_t = tab_ref[:, 3 * LANES:4 * LANES]
    kpe = proj_s[:, OFF_KPA:OFF_KPA + LANES] * ck_t + proj_s[:, OFF_KPB:OFF_KPB + LANES] * sk_t
    return ckv_n, kpe


def _q_head(qab_s, tab_ref, h):
    cq_t = tab_ref[:, 0:LANES]
    sq_t = tab_ref[:, LANES:2 * LANES]
    lo = h * LANES
    hi = N_HEADS * LANES + h * LANES
    return qab_s[:, lo:lo + LANES] * cq_t + qab_s[:, hi:hi + LANES] * sq_t


def _lru_terms(xc, wg_ref, bg_ref, la_ref):
    gates = _dot(xc.astype(BF16), wg_ref[...]) + bg_ref[...]
    r = jax.nn.sigmoid(gates[:, 0:D_RNN])
    i = jax.nn.sigmoid(gates[:, D_RNN:2 * D_RNN])
    log_a = (-LRU_C * r) * _softplus(-la_ref[...])
    a = jnp.exp(log_a)
    mult = jnp.sqrt(-jnp.tanh(log_a) * (a * a + 1.0))
    return a, mult * (i * xc)


def _in_prompt_kernel(x_ref, tab_ref, h0_ref, c0_ref,
                      nm_ref, win_ref, qn_ref, wuq_ref, kvn_ref, wkv_ref,
                      cw_ref, cb_ref, wg_ref, bg_ref, la_ref,
                      ckv_ref, kpe_ref, q_ref, k_ref, v_ref, rnn_ref, hf_ref, cf_ref,
                      proj_s, qab_s, kv_s, ubuf, a_s, b_s, h_s, hc_s, *, tm, n_pad):
    t = pl.program_id(1)
    ckv_n, kpe = _front(x_ref, tab_ref, nm_ref, win_ref, qn_ref, wuq_ref, kvn_ref, proj_s, qab_s)
    ckv_ref[...] = ckv_n
    kpe_ref[...] = kpe
    kv_s[...] = _dot(ckv_n.astype(BF16), wkv_ref[...])
    for h in range(N_HEADS):
        sl = slice(h * LANES, (h + 1) * LANES)
        q_ref[:, sl] = _q_head(qab_s, tab_ref, h).astype(BF16)
        k_ref[:, sl] = (kv_s[:, sl] + kpe).astype(BF16)
    v_ref[...] = kv_s[:, N_HEADS * LANES:KV_COLS].astype(BF16)

    @pl.when(t == 0)
    def _():
        ubuf[0:SUBLANES, :] = c0_ref[...]
        hc_s[...] = h0_ref[...]

    @pl.when(t > 0)
    def _():
        ubuf[0:SUBLANES, :] = ubuf[tm:tm + SUBLANES, :]

    u = proj_s[:, OFF_U:OFF_U + D_RNN]
    ubuf[SUBLANES:SUBLANES + tm, :] = u
    cw = cw_ref[...]
    xc = cb_ref[...]
    for k in range(CONV_W - 1):
        lo = SUBLANES - (CONV_W - 1) + k
        xc = xc + ubuf[lo:lo + tm, :] * cw[k:k + 1, :]
    xc = xc + u * cw[CONV_W - 1:CONV_W, :]
    a, b = _lru_terms(xc, wg_ref, bg_ref, la_ref)
    row = lax.broadcasted_iota(jnp.int32, (tm, D_RNN), 0)
    if n_pad:
        b = jnp.where(row + t * tm >= n_pad, b, 0.0)

    sub = row & (SUBLANES - 1)
    for d in (1, 2, 4):
        keep = sub >= d
        b = jnp.where(keep, a * pltpu.roll(b, d, 0) + b, b)
        a = jnp.where(keep, a * pltpu.roll(a, d, 0), a)
    a_s[...] = a
    b_s[...] = b

    def group(gi, h_in):
        o = pl.multiple_of(gi * SUBLANES, SUBLANES)
        hr = a_s[pl.ds(o, SUBLANES), :] * h_in + b_s[pl.ds(o, SUBLANES), :]
        h_s[pl.ds(o, SUBLANES), :] = hr
        return hr[SUBLANES - 1:SUBLANES, :]

    h_last = lax.fori_loop(0, tm // SUBLANES, group, hc_s[...], unroll=8)
    hc_s[...] = h_last
    hf_ref[...] = h_last
    cf_ref[...] = ubuf[tm:tm + SUBLANES, :]
    rnn_ref[...] = (h_s[...] * _gelu_tanh(proj_s[:, OFF_G:OFF_G + D_RNN])).astype(BF16)


def _in_prompt(x, tab, h0, c0, w, *, tm, n_pad):
    nb, L, _ = x.shape
    nt = L // tm
    assert nt * tm == L
    bstate = (lambda b, t: (b, 0, 0)) if h0.shape[0] == nb else (lambda b, t: (0, 0, 0))
    row_spec = lambda c: pl.BlockSpec((None, tm, c), lambda b, t: (b, t, 0))
    in_specs = [
        row_spec(D_MODEL),
        pl.BlockSpec((tm, 4 * LANES), lambda b, t: (t, 0)),
        pl.BlockSpec((None, 1, D_RNN), bstate),
        pl.BlockSpec((None, SUBLANES, D_RNN), bstate),
        _const_spec((1, D_MODEL)), _const_spec((D_MODEL, IN_COLS)),
        _const_spec((1, Q_LORA)), _const_spec((Q_LORA, QAB_COLS)),
        _const_spec((1, KV_LORA)), _const_spec((KV_LORA, KV_COLS)),
        _const_spec((CONV_W, D_RNN)), _const_spec((1, D_RNN)),
        _const_spec((D_RNN, 2 * D_RNN)), _const_spec((1, 2 * D_RNN)), _const_spec((1, D_RNN)),
    ]
    out_shape = [
        jax.ShapeDtypeStruct((nb, L, KV_LORA), F32),
        jax.ShapeDtypeStruct((nb, L, LANES), F32),
        jax.ShapeDtypeStruct((nb, L, N_HEADS * LANES), BF16),
        jax.ShapeDtypeStruct((nb, L, N_HEADS * LANES), BF16),
        jax.ShapeDtypeStruct((nb, L, N_HEADS * V_DIM), BF16),
        jax.ShapeDtypeStruct((nb, L, D_RNN), BF16),
        jax.ShapeDtypeStruct((nb, 1, D_RNN), F32),
        jax.ShapeDtypeStruct((nb, SUBLANES, D_RNN), F32),
    ]
    out_specs = [
        row_spec(KV_LORA), row_spec(LANES), row_spec(N_HEADS * LANES), row_spec(N_HEADS * LANES),
        row_spec(N_HEADS * V_DIM), row_spec(D_RNN),
        pl.BlockSpec((None, 1, D_RNN), lambda b, t: (b, 0, 0)),
        pl.BlockSpec((None, SUBLANES, D_RNN), lambda b, t: (b, 0, 0)),
    ]
    scratch = [
        pltpu.VMEM((tm, IN_COLS), F32), pltpu.VMEM((tm, QAB_COLS), F32), pltpu.VMEM((tm, KV_COLS), F32),
        pltpu.VMEM((tm + SUBLANES, D_RNN), F32),
        pltpu.VMEM((tm, D_RNN), F32), pltpu.VMEM((tm, D_RNN), F32), pltpu.VMEM((tm, D_RNN), F32),
        pltpu.VMEM((1, D_RNN), F32),
    ]
    return pl.pallas_call(
        functools.partial(_in_prompt_kernel, tm=tm, n_pad=n_pad),
        grid=(nb, nt), in_specs=in_specs, out_specs=out_specs, out_shape=out_shape,
        scratch_shapes=scratch, name="in_prompt",
        compiler_params=pltpu.CompilerParams(
            dimension_semantics=("arbitrary", "arbitrary"), vmem_limit_bytes=VMEM_LIMIT),
    )(x, tab, h0, c0, w["norm_mix"], w["w_in"], w["q_norm"], w["w_uq"], w["kv_norm"], w["w_kv"],
      w["conv_w"], w["conv_b"], w["w_gates"], w["b_gates"], w["lru_a"])


def _in_sample_kernel(x_ref, tab_ref, h0_ref, cp_ref,
                      nm_ref, win_ref, qn_ref, wuq_ref, kvn_ref, wukt_ref,
                      cw_ref, cb_ref, wg_ref, bg_ref, la_ref,
                      ckv_ref, kpe_ref, qlat_ref, qrot_ref, rnn_ref, hf_ref, cf_ref,
                      proj_s, qab_s, xc_s, h_s, *, nb, nt):
    ckv_n, kpe = _front(x_ref, tab_ref, nm_ref, win_ref, qn_ref, wuq_ref, kvn_ref, proj_s, qab_s)
    ckv_ref[...] = ckv_n
    kpe_ref[...] = kpe
    for h in range(N_HEADS):
        qh = _q_head(qab_s, tab_ref, h).astype(BF16)
        qrot_ref[:, h * LANES:(h + 1) * LANES] = qh
        qlat_ref[h] = _dot(qh, wukt_ref[h]).astype(BF16)

    cw = cw_ref[...]

    def xpad(j):
        if j < CONV_W - 1:
            return cp_ref[j]
        jj = j - (CONV_W - 1)
        return proj_s[jj * nb:(jj + 1) * nb, OFF_U:OFF_U + D_RNN]

    for t in range(nt):
        xc = cb_ref[...]
        for k in range(CONV_W):
            xc = xc + xpad(t + k) * cw[k:k + 1, :]
        xc_s[t * nb:(t + 1) * nb, :] = xc
    a, b = _lru_terms(xc_s[...], wg_ref, bg_ref, la_ref)
    h = h0_ref[...]
    for t in range(nt):
        h = a[t * nb:(t + 1) * nb, :] * h + b[t * nb:(t + 1) * nb, :]
        h_s[t * nb:(t + 1) * nb, :] = h
    hf_ref[...] = h
    for j in range(CONV_W - 1):
        cf_ref[j] = xpad(nt + j)
    rnn_ref[...] = (h_s[...] * _gelu_tanh(proj_s[:, OFF_G:OFF_G + D_RNN])).astype(BF16)


def _in_sample(x, tab, h0, cp, w, *, nb, nt):
    n = nb * nt
    full = lambda shape: pl.BlockSpec(shape, lambda i: (0,) * len(shape))
    in_specs = [
        full((n, D_MODEL)), full((n, 4 * LANES)), full((nb, D_RNN)), full((CONV_W - 1, nb, D_RNN)),
        full((1, D_MODEL)), full((D_MODEL, IN_COLS)), full((1, Q_LORA)), full((Q_LORA, QAB_COLS)),
        full((1, KV_LORA)), full((N_HEADS, LANES, KV_LORA)),
        full((CONV_W, D_RNN)), full((1, D_RNN)), full((D_RNN, 2 * D_RNN)), full((1, 2 * D_RNN)),
        full((1, D_RNN)),
    ]
    out_shape = [
        jax.ShapeDtypeStruct((n, KV_LORA), F32),
        jax.ShapeDtypeStruct((n, LANES), F32),
        jax.ShapeDtypeStruct((N_HEADS, n, KV_LORA), BF16),
        jax.ShapeDtypeStruct((n, N_HEADS * LANES), BF16),
        jax.ShapeDtypeStruct((n, D_RNN), BF16),
        jax.ShapeDtypeStruct((nb, D_RNN), F32),
        jax.ShapeDtypeStruct((CONV_W - 1, nb, D_RNN), F32),
    ]
    out_specs = [full(s.shape) for s in out_shape]
    scratch = [
        pltpu.VMEM((n, IN_COLS), F32), pltpu.VMEM((n, QAB_COLS), F32),
        pltpu.VMEM((n, D_RNN), F32), pltpu.VMEM((n, D_RNN), F32),
    ]
    return pl.pallas_call(
        functools.partial(_in_sample_kernel, nb=nb, nt=nt),
        grid=(1,), in_specs=in_specs, out_specs=out_specs, out_shape=out_shape,
        scratch_shapes=scratch, name="in_sample",
        compiler_params=pltpu.CompilerParams(
            dimension_semantics=("arbitrary",), vmem_limit_bytes=VMEM_LIMIT),
    )(x, tab, h0, cp, w["norm_mix"], w["w_in"], w["q_norm"], w["w_uq"], w["kv_norm"], w["w_ukt"],
      w["conv_w"], w["conv_b"], w["w_gates"], w["b_gates"], w["lru_a"])


def _softmax_init(s, v):
    m = jnp.max(s, axis=-1, keepdims=True)
    p = jnp.exp(s - m)
    return m, jnp.sum(p, axis=-1, keepdims=True), _dot(p.astype(BF16), v)


def _merge_pair(st0, st1):
    o0 = st0[2] / st0[1]
    o1 = st1[2] / st1[1]
    lane = lax.broadcasted_iota(jnp.int32, o0.shape, 1)
    return jnp.where(lane < V_DIM, o0, o1).astype(BF16)


def _attn_meta_kernel(q_ref, k_ref, v_ref, o_ref, *, n_pad):
    n = q_ref.shape[0]
    row = lax.broadcasted_iota(jnp.int32, (n, n), 0)
    col = lax.broadcasted_iota(jnp.int32, (n, n), 1)
    mask = (col <= row) & ((col >= n_pad) | (row < n_pad))
    v = v_ref[...]
    sts = []
    for hh in range(2):
        sl = slice(hh * LANES, (hh + 1) * LANES)
        s = jnp.where(mask, _dot_nt(q_ref[:, sl], k_ref[:, sl]), -jnp.inf)
        sts.append(_softmax_init(s, v))
    o_ref[...] = _merge_pair(*sts)


def _attn_meta(q, k, v, *, n_pad):
    n = q.shape[0]
    return pl.pallas_call(
        functools.partial(_attn_meta_kernel, n_pad=n_pad),
        grid=(N_HEADS // 2,),
        in_specs=[pl.BlockSpec((n, 2 * LANES), lambda p: (0, p)),
                  pl.BlockSpec((n, 2 * LANES), lambda p: (0, p)),
                  pl.BlockSpec((n, LANES), lambda p: (0, p))],
        out_specs=pl.BlockSpec((n, LANES), lambda p: (0, p)),
        out_shape=jax.ShapeDtypeStruct((n, N_HEADS * V_DIM), BF16),
        name="attn_meta",
        compiler_params=pltpu.CompilerParams(dimension_semantics=("arbitrary",)),
    )(q, k, v)


def _attn_main_kernel(q_ref, k_ref, v_ref, kp_ref, vp_ref, o_ref, *, tq, n_pad):
    L = q_ref.shape[0]
    npre = kp_ref.shape[0]
    pre_ok = lax.broadcasted_iota(jnp.int32, (tq, npre), 1) >= n_pad
    causal = (lax.broadcasted_iota(jnp.int32, (tq, tq), 1)
              <= lax.broadcasted_iota(jnp.int32, (tq, tq), 0))
    rowmax = lambda s: jnp.max(s, axis=-1, keepdims=True)
    rowsum = lambda p: jnp.sum(p, axis=-1, keepdims=True)
    vp = vp_ref[...]
    nq = L // tq
    for qb0 in range(0, nq, ATTN_QGROUP):
        chains = [(qb, hh) for qb in range(qb0, min(qb0 + ATTN_QGROUP, nq)) for hh in range(2)]
        scores = []
        for qb, hh in chains:
            r0, n = qb * tq, (qb + 1) * tq
            hs = slice(hh * LANES, (hh + 1) * LANES)
            q = q_ref[r0:n, hs]
            sp = jnp.where(pre_ok, _dot_nt(q, kp_ref[:, hs]), -jnp.inf)
            sd = jnp.where(causal, _dot_nt(q, k_ref[r0:n, hs]), -jnp.inf)
            sm = _dot_nt(q, k_ref[0:r0, hs]) if qb else None
            scores.append((sp, sd, sm))
        probs = []
        for sp, sd, sm in scores:
            m = jnp.maximum(rowmax(sp), rowmax(sd))
            if sm is not None:
                m = jnp.maximum(m, rowmax(sm))
            pp = jnp.exp(sp - m)
            pd = jnp.exp(sd - m)
            l = rowsum(pp) + rowsum(pd)
            pm = None
            if sm is not None:
                pm = jnp.exp(sm - m)
                l = l + rowsum(pm)
                pm = pm.astype(BF16)
            probs.append((m, l, pp.astype(BF16), pd.astype(BF16), pm))
        sts = []
        for (qb, hh), (m, l, pp, pd, pm) in zip(chains, probs):
            r0, n = qb * tq, (qb + 1) * tq
            acc = _dot(pp, vp) + _dot(pd, v_ref[r0:n, :])
            if pm is not None:
                acc = acc + _dot(pm, v_ref[0:r0, :])
            sts.append((m, l, acc))
        for i in range(0, len(chains), 2):
            r0 = chains[i][0] * tq
            o_ref[r0:r0 + tq, :] = _merge_pair(sts[i], sts[i + 1])


def _attn_main(q, k, v, kpre, vpre, *, n_pad):
    B, L, _ = q.shape
    npre = kpre.shape[0]
    tq = min(ATTN_TQ, L)
    assert L % tq == 0
    return pl.pallas_call(
        functools.partial(_attn_main_kernel, tq=tq, n_pad=n_pad),
        grid=(B, N_HEADS // 2),
        in_specs=[pl.BlockSpec((None, L, 2 * LANES), lambda b, p: (b, 0, p)),
                  pl.BlockSpec((None, L, 2 * LANES), lambda b, p: (b, 0, p)),
                  pl.BlockSpec((None, L, LANES), lambda b, p: (b, 0, p)),
                  pl.BlockSpec((npre, 2 * LANES), lambda b, p: (0, p)),
                  pl.BlockSpec((npre, LANES), lambda b, p: (0, p))],
        out_specs=pl.BlockSpec((None, L, LANES), lambda b, p: (b, 0, p)),
        out_shape=jax.ShapeDtypeStruct((B, L, N_HEADS * V_DIM), BF16),
        name="attn_main",
        compiler_params=pltpu.CompilerParams(
            dimension_semantics=("arbitrary", "arbitrary"), vmem_limit_bytes=VMEM_LIMIT),
    )(q, k, v, kpre, vpre)


def _decode_kernel(pt_ref, ql_ref, qp_ref, cn_ref, kn_ref, ckv_hbm, kpe_hbm, o_ref,
                   cbuf, kbuf, ncbuf, nkbuf, sem, m_s, l_s, acc_s, *, layer, G, NC, T):
    b = pl.program_id(0)
    c = pl.program_id(1)
    n = b * NC + c
    total = pl.num_programs(0) * NC
    slot = lax.rem(n, 2)

    def copies(bb, cc, sl):
        out = []
        for g in range(G):
            page = pt_ref[bb, cc * G + g]
            out.append(pltpu.make_async_copy(ckv_hbm.at[layer, page], cbuf.at[sl, g], sem.at[0, sl]))
            out.append(pltpu.make_async_copy(kpe_hbm.at[layer, page], kbuf.at[sl, g], sem.at[1, sl]))
        return out

    @pl.when(n == 0)
    def _():
        ncbuf[...] = jnp.zeros_like(ncbuf)
        nkbuf[...] = jnp.zeros_like(nkbuf)
        for cp in copies(b, c, slot):
            cp.start()

    @pl.when(n + 1 < total)
    def _():
        n1 = n + 1
        for cp in copies(n1 // NC, lax.rem(n1, NC), 1 - slot):
            cp.start()

    pltpu.make_async_copy(ckv_hbm.at[layer, pl.ds(0, G)], cbuf.at[slot], sem.at[0, slot]).wait()
    pltpu.make_async_copy(kpe_hbm.at[layer, pl.ds(0, G)], kbuf.at[slot], sem.at[1, slot]).wait()

    ql = ql_ref[...].astype(F32)
    qp = qp_ref[...].astype(F32)

    @pl.when(c == 0)
    def _():
        m_s[...] = jnp.full_like(m_s, -jnp.inf)
        l_s[...] = jnp.zeros_like(l_s)
        acc_s[...] = jnp.zeros_like(acc_s)

    def partial_softmax(kvs, kpts, mask_fn=None):
        ss = [_dot_nt(ql, kv) + _dot(qp, kpt) for kv, kpt in zip(kvs, kpts)]
        if mask_fn is not None:
            ss = [mask_fn(s) for s in ss]
        ms = [jnp.max(s, axis=-1, keepdims=True) for s in ss]
        ps = [jnp.exp(s - m) for s, m in zip(ss, ms)]
        ls = [jnp.sum(p, axis=-1, keepdims=True) for p in ps]
        accs = [_dot(p, kv) for p, kv in zip(ps, kvs)]
        return list(zip(ms, ls, accs))

    def merge(parts):
        m_old = m_s[...]
        m_new = m_old
        for m, _, _ in parts:
            m_new = jnp.maximum(m_new, m)
        w_old = jnp.exp(m_old - m_new)
        l = w_old * l_s[...]
        acc = w_old * acc_s[...]
        for m, li, ai in parts:
            wi = jnp.exp(m - m_new)
            l = l + wi * li
            acc = acc + wi * ai
        return m_new, l, acc

    pp = G // DEC_SPLIT
    kvs = [cbuf[slot, sc * pp:(sc + 1) * pp].reshape(pp * PAGE_SIZE, KV_LORA) for sc in range(DEC_SPLIT)]
    kpts = [jnp.concatenate([kbuf[slot, sc * pp + g] for g in range(pp)], axis=1)
            for sc in range(DEC_SPLIT)]
    m_s[...], l_s[...], acc_s[...] = merge(partial_softmax(kvs, kpts))

    @pl.when(c == NC - 1)
    def _():
        ncbuf[0:T, :] = cn_ref[...]
        nkbuf[:, 0:T] = kn_ref[...]

        def mask(s):
            tq = lax.rem(lax.broadcasted_iota(jnp.int32, s.shape, 0), T)
            col = lax.broadcasted_iota(jnp.int32, s.shape, 1)
            return jnp.where(col <= tq, s, -jnp.inf)

        m, l, acc = merge(partial_softmax([ncbuf[...]], [nkbuf[...]], mask))
        o_ref[...] = acc / l


def _decode_attn(page_table, qlat, qpe, ckv_new, kpe_new_t, cache_ckv, cache_kpe_t, *, layer):
    DB, rows, _ = qlat.shape
    T = ckv_new.shape[1]
    n_pages = page_table.shape[1]
    G = min(DEC_PAGES, n_pages)
    assert n_pages % G == 0 and G % DEC_SPLIT == 0
    NC = n_pages // G
    grid_spec = pltpu.PrefetchScalarGridSpec(
        num_scalar_prefetch=1,
        grid=(DB, NC),
        in_specs=[
            pl.BlockSpec((None, rows, KV_LORA), lambda b, c, pt: (b, 0, 0)),
            pl.BlockSpec((None, rows, QK_ROPE), lambda b, c, pt: (b, 0, 0)),
            pl.BlockSpec((None, T, KV_LORA), lambda b, c, pt: (b, 0, 0)),
            pl.BlockSpec((None, QK_ROPE, T), lambda b, c, pt: (b, 0, 0)),
            pl.BlockSpec(memory_space=pl.ANY),
            pl.BlockSpec(memory_space=pl.ANY),
        ],
        out_specs=pl.BlockSpec((None, rows, KV_LORA), lambda b, c, pt: (b, 0, 0)),
        scratch_shapes=[
            pltpu.VMEM((2, G, PAGE_SIZE, KV_LORA), F32),
            pltpu.VMEM((2, G, QK_ROPE, PAGE_SIZE), F32),
            pltpu.VMEM((PAGE_SIZE, KV_LORA), F32),
            pltpu.VMEM((QK_ROPE, PAGE_SIZE), F32),
            pltpu.SemaphoreType.DMA((2, 2)),
            pltpu.VMEM((rows, 1), F32), pltpu.VMEM((rows, 1), F32), pltpu.VMEM((rows, KV_LORA), F32),
        ],
    )
    return pl.pallas_call(
        functools.partial(_decode_kernel, layer=layer, G=G, NC=NC, T=T),
        grid_spec=grid_spec,
        out_shape=jax.ShapeDtypeStruct((DB, rows, KV_LORA), F32),
        name="decode_attn",
        compiler_params=pltpu.CompilerParams(
            dimension_semantics=("arbitrary", "arbitrary"), vmem_limit_bytes=VMEM_LIMIT),
    )(page_table, qlat, qpe, ckv_new, kpe_new_t, cache_ckv, cache_kpe_t)


def _ffn_tail(x1, nf_ref, wg_ref, wu_ref, wd_ref, fn_ref, o_ref, acc_s, final):
    xn = _rms(x1, nf_ref[...]).astype(BF16)
    for ci in range(D_FF // FF_CHUNK):
        sl = slice(ci * FF_CHUNK, (ci + 1) * FF_CHUNK)
        hg = _dot(xn, wg_ref[:, sl])
        hu = _dot(xn, wu_ref[:, sl])
        act = ((hg * jax.nn.sigmoid(hg)) * hu).astype(BF16)
        part = _dot(act, wd_ref[sl, :])
        if ci == 0:
            acc_s[...] = part
        else:
            acc_s[...] += part
    x2 = x1 + acc_s[...]
    if final:
        x2 = _rms(x2, fn_ref[...])
    o_ref[...] = x2


def _out_kernel(x_ref, a_ref, r_ref, wo_ref, nf_ref, wg_ref, wu_ref, wd_ref, fn_ref, o_ref, acc_s,
                *, final):
    na = a_ref.shape[1]
    y = _dot(a_ref[...], wo_ref[0:na, :]) + _dot(r_ref[...], wo_ref[na:, :])
    _ffn_tail(x_ref[...] + y, nf_ref, wg_ref, wu_ref, wd_ref, fn_ref, o_ref, acc_s, final)


def _out_latent_kernel(x_ref, ol_ref, wuv_ref, r_ref, wo_ref, nf_ref, wg_ref, wu_ref, wd_ref, fn_ref,
                       o_ref, acc_s, *, final):
    na = N_HEADS * V_DIM
    y = _dot(r_ref[...], wo_ref[na:, :])
    for h in range(N_HEADS):
        ah = _dot(ol_ref[h], wuv_ref[h]).astype(BF16)
        y = y + _dot(ah, wo_ref[h * V_DIM:(h + 1) * V_DIM, :])
    _ffn_tail(x_ref[...] + y, nf_ref, wg_ref, wu_ref, wd_ref, fn_ref, o_ref, acc_s, final)


def _out_proj(x, attn, rnn, w, final_norm, *, tm, final, latent=False):
    n = x.shape[0]
    assert n % tm == 0
    row = lambda c: pl.BlockSpec((tm, c), lambda i: (i, 0))
    wspecs = [_const_spec((D_MODEL, D_MODEL)), _const_spec((1, D_MODEL)),
              _const_spec((D_MODEL, D_FF)), _const_spec((D_MODEL, D_FF)), _const_spec((D_FF, D_MODEL)),
              _const_spec((1, D_MODEL))]
    wargs = (w["w_out"], w["norm_ffn"], w["w_gate"], w["w_up"], w["w_down"], final_norm)
    if latent:
        kern = functools.partial(_out_latent_kernel, final=final)
        in_specs = [row(D_MODEL), pl.BlockSpec((N_HEADS, tm, KV_LORA), lambda i: (0, i, 0)),
                    _const_spec((N_HEADS, KV_LORA, V_DIM)), row(D_RNN)] + wspecs
        args = (x, attn, w["w_uv3"], rnn) + wargs
    else:
        kern = functools.partial(_out_kernel, final=final)
        in_specs = [row(D_MODEL), row(N_HEADS * V_DIM), row(D_RNN)] + wspecs
        args = (x, attn, rnn) + wargs
    return pl.pallas_call(
        kern, grid=(n // tm,), in_specs=in_specs, out_specs=row(D_MODEL),
        out_shape=jax.ShapeDtypeStruct((n, D_MODEL), F32),
        scratch_shapes=[pltpu.VMEM((tm, D_MODEL), F32)],
        name="out_latent" if latent else "out_proj",
        compiler_params=pltpu.CompilerParams(
            dimension_semantics=("arbitrary",), vmem_limit_bytes=VMEM_LIMIT),
    )(*args)


def _rot_half_cols(w):
    half = QK_ROPE // 2
    return jnp.concatenate([-w[..., half:], w[..., :half]], axis=-1)


def _prep_layer(l, p):
    w_in = p["w_in"][l]
    d = w_in.shape[0]
    o_kpe = Q_LORA + KV_LORA
    kpe = w_in[:, o_kpe:o_kpe + QK_ROPE]
    z = lambda c: jnp.zeros((d, c), F32)
    w_in_r = jnp.concatenate(
        [w_in[:, :o_kpe], w_in[:, o_kpe + QK_ROPE:],
         z(QK_NOPE), kpe, z(LANES - QK_NOPE - QK_ROPE),
         z(QK_NOPE), _rot_half_cols(kpe), z(LANES - QK_NOPE - QK_ROPE)], axis=1)

    w_uq = p["w_uq"][l].reshape(Q_LORA, N_HEADS, QK_NOPE + QK_ROPE)
    nope, pe = w_uq[..., :QK_NOPE], w_uq[..., QK_NOPE:]
    zq = lambda c: jnp.zeros((Q_LORA, N_HEADS, c), F32)
    qa = jnp.concatenate([nope, pe, zq(LANES - QK_NOPE - QK_ROPE)], axis=-1)
    qb = jnp.concatenate([zq(QK_NOPE), _rot_half_cols(pe), zq(LANES - QK_NOPE - QK_ROPE)], axis=-1)
    w_uq_r = jnp.concatenate([qa.reshape(Q_LORA, -1), qb.reshape(Q_LORA, -1)], axis=1)

    w_uk = p["w_uk"][l]
    w_uv = p["w_uv"][l]
    w_k = jnp.concatenate([w_uk, jnp.zeros((KV_LORA, N_HEADS, LANES - QK_NOPE), F32)], axis=-1)
    w_kv = jnp.concatenate([w_k.reshape(KV_LORA, -1), w_uv.reshape(KV_LORA, -1)], axis=1)
    w_ukt = jnp.concatenate(
        [jnp.transpose(w_uk, (1, 2, 0)), jnp.zeros((N_HEADS, LANES - QK_NOPE, KV_LORA), F32)], axis=1)

    eye = jnp.eye(N_RNN_BLOCKS, dtype=F32)
    bd = lambda wb: jnp.einsum("nde,nm->ndme", wb, eye).reshape(D_RNN, D_RNN)
    w_gates = jnp.concatenate([bd(p["w_ra"][l]), bd(p["w_ri"][l])], axis=1)
    row = lambda v: v.reshape(1, -1).astype(F32)
    return {
        "norm_mix": row(p["norm_mix"][l]), "w_in": w_in_r.astype(BF16),
        "q_norm": row(p["q_norm"][l]), "w_uq": w_uq_r.astype(BF16),
        "kv_norm": row(p["kv_norm"][l]), "w_kv": w_kv.astype(BF16), "w_ukt": w_ukt.astype(BF16),
        "w_uv3": jnp.transpose(w_uv, (1, 0, 2)).astype(BF16),
        "conv_w": p["conv_w"][l].astype(F32), "conv_b": row(p["conv_b"][l]),
        "w_gates": w_gates.astype(BF16),
        "b_gates": jnp.concatenate([p["b_ra"][l], p["b_ri"][l]]).reshape(1, -1).astype(F32),
        "lru_a": row(p["lru_a"][l]),
        "w_out": p["w_out"][l].astype(BF16), "norm_ffn": row(p["norm_ffn"][l]),
        "w_gate": p["w_gate"][l].astype(BF16), "w_up": p["w_up"][l].astype(BF16),
        "w_down": p["w_down"][l].astype(BF16),
    }


def _rope_table(pos):
    half = QK_ROPE // 2
    inv = ROPE_THETA ** (-jnp.arange(half, dtype=F32) / half)
    ang = pos[:, None] * inv[None, :]
    cos = jnp.tile(jnp.cos(ang), (1, 2))
    sin = jnp.tile(jnp.sin(ang), (1, 2))
    n = pos.shape[0]
    one = jnp.ones((n, QK_NOPE), F32)
    z_lo = jnp.zeros((n, QK_NOPE), F32)
    z_hi = jnp.zeros((n, LANES - QK_NOPE - QK_ROPE), F32)
    return jnp.concatenate(
        [one * ATTN_SCALE, cos * ATTN_SCALE, z_hi, z_lo, sin * ATTN_SCALE, z_hi,
         z_lo, cos, z_hi, z_lo, sin, z_hi], axis=1)


def _pick_tile(n, pref):
    t = min(pref, n)
    while n % t:
        t //= 2
    return t


def kernel(x_prompt, x_sample, cache_ckv, cache_kpe, state_h, state_conv, page_table, meta_tokens,
           norm_mix, w_in, q_norm, w_uq, kv_norm, w_uk, w_uv, conv_w, conv_b, w_ra, b_ra, w_ri, b_ri,
           lru_a, w_out, norm_ffn, w_gate, w_up, w_down, final_norm):
    params = dict(norm_mix=norm_mix, w_in=w_in, q_norm=q_norm, w_uq=w_uq, kv_norm=kv_norm, w_uk=w_uk,
                  w_uv=w_uv, conv_w=conv_w, conv_b=conv_b, w_ra=w_ra, b_ra=b_ra, w_ri=w_ri, b_ri=b_ri,
                  lru_a=lru_a, w_out=w_out, norm_ffn=norm_ffn, w_gate=w_gate, w_up=w_up, w_down=w_down)
    depth = w_in.shape[0]
    B, S, _ = x_prompt.shape
    DB, T, _ = x_sample.shape
    assert T >= CONV_W - 1
    past_len = page_table.shape[1] * PAGE_SIZE
    fnorm = final_norm.reshape(1, -1).astype(F32)

    tab_meta = _rope_table(jnp.arange(META_ROWS, dtype=F32) - META_PAD)
    tab_main = _rope_table(N_META + jnp.arange(S, dtype=F32))
    tab_samp = _rope_table(jnp.repeat(past_len + jnp.arange(T, dtype=F32), DB))

    xm = jnp.concatenate([jnp.zeros((META_PAD, D_MODEL), F32), meta_tokens.astype(F32)], axis=0)[None]
    xp = x_prompt
    xs = jnp.transpose(x_sample, (1, 0, 2)).reshape(T * DB, D_MODEL)
    zero_h = jnp.zeros((1, 1, D_RNN), F32)
    zero_c = jnp.zeros((1, SUBLANES, D_RNN), F32)
    tm_in = _pick_tile(S, 512)
    tm_out = _pick_tile(B * S, 512)
    tm_s = _pick_tile(T * DB, 512)

    cache_kpe_t = jnp.transpose(cache_kpe, (0, 1, 3, 2))
    outs = {k: [] for k in ("ckv_p", "kpe_p", "h_p", "cv_p", "ckv_s", "kpe_s", "h_s", "cv_s")}
    rope_lanes = slice(QK_NOPE, QK_NOPE + QK_ROPE)
    for l in range(depth):
        w = _prep_layer(l, params)
        last = l == depth - 1

        ckv_m, kpe_m, q_m, k_m, v_m, rnn_m, h_m, c_m = _in_prompt(
            xm, tab_meta, zero_h, zero_c, w, tm=META_ROWS, n_pad=META_PAD)
        attn_m = _attn_meta(q_m[0], k_m[0], v_m[0], n_pad=META_PAD)
        xm = _out_proj(xm[0], attn_m, rnn_m[0], w, fnorm, tm=META_ROWS, final=False)[None]

        ckv, kpe, q, k, v, rnn, h_f, c_f = _in_prompt(xp, tab_main, h_m, c_m, w, tm=tm_in, n_pad=0)
        attn = _attn_main(q, k, v, k_m[0], v_m[0], n_pad=META_PAD)
        xp = _out_proj(xp.reshape(B * S, D_MODEL), attn.reshape(B * S, -1), rnn.reshape(B * S, -1),
                       w, fnorm, tm=tm_out, final=last).reshape(B, S, D_MODEL)
        bc = lambda a: jnp.broadcast_to(a[None], (B,) + a.shape)
        outs["ckv_p"].append(jnp.concatenate([bc(ckv_m[0, META_PAD:]), ckv], axis=1))
        outs["kpe_p"].append(jnp.concatenate(
            [bc(kpe_m[0, META_PAD:, rope_lanes]), kpe[..., rope_lanes]], axis=1))
        outs["h_p"].append(h_f[:, 0])
        outs["cv_p"].append(c_f[:, SUBLANES - (CONV_W - 1):])

        cp = jnp.transpose(state_conv[l], (1, 0, 2))
        ckv_s, kpe_s, qlat, qrot, rnn_s, h_s, cv_s = _in_sample(
            xs, tab_samp, state_h[l], cp, w, nb=DB, nt=T)
        ckv_new = jnp.transpose(ckv_s.reshape(T, DB, KV_LORA), (1, 0, 2))
        kpe_new = jnp.transpose(kpe_s[:, rope_lanes].reshape(T, DB, QK_ROPE), (1, 0, 2))
        ql = jnp.transpose(qlat.reshape(N_HEADS, T, DB, KV_LORA), (2, 0, 1, 3)).reshape(DB, N_HEADS * T, KV_LORA)
        qp = qrot.reshape(T, DB, N_HEADS, LANES)[..., rope_lanes]
        qp = jnp.transpose(qp, (1, 2, 0, 3)).reshape(DB, N_HEADS * T, QK_ROPE)
        o_lat = _decode_attn(page_table, ql, qp, ckv_new, jnp.transpose(kpe_new, (0, 2, 1)),
                             cache_ckv, cache_kpe_t, layer=l)
        ol = jnp.transpose(o_lat.reshape(DB, N_HEADS, T, KV_LORA), (1, 2, 0, 3))
        ol = ol.reshape(N_HEADS, T * DB, KV_LORA).astype(BF16)
        xs = _out_proj(xs, ol, rnn_s, w, fnorm, tm=tm_s, final=last, latent=True)
        outs["ckv_s"].append(ckv_new)
        outs["kpe_s"].append(kpe_new)
        outs["h_s"].append(h_s)
        outs["cv_s"].append(jnp.transpose(cv_s, (1, 0, 2)))

    y_sample = jnp.transpose(xs.reshape(T, DB, D_MODEL), (1, 0, 2))
    st = lambda k: jnp.stack(outs[k])
    return (xp, y_sample, st("ckv_p"), st("kpe_p"), st("h_p"), st("cv_p"),
            st("ckv_s"), st("kpe_s"), st("h_s"), st("cv_s"))
```

```python
import functools
import math

import jax
import jax.numpy as jnp
from jax import lax
from jax.experimental import pallas as pl
from jax.experimental.pallas import tpu as pltpu

F32 = jnp.float32
BF16 = jnp.bfloat16

D_MODEL = 1024
N_META = 16
N_HEADS = 8
QK_NOPE = 64
QK_ROPE = 32
V_DIM = 64
Q_LORA = 384
KV_LORA = 256
ROPE_THETA = 10000.0
D_RNN = 512
N_RNN_BLOCKS = 8
RNN_BLOCK = D_RNN // N_RNN_BLOCKS
CONV_W = 4
LRU_C = 8.0
D_FF = 2816
RMS_EPS = 1e-6
PAGE_SIZE = 128
ATTN_SCALE = 1.0 / math.sqrt(QK_NOPE + QK_ROPE)

LANES = 128
SUBLANES = 8
VMEM_LIMIT = 56 * 1024 * 1024

OFF_CQ = 0
OFF_CKV = OFF_CQ + Q_LORA
OFF_U = OFF_CKV + KV_LORA
OFF_G = OFF_U + D_RNN
OFF_KPA = OFF_G + D_RNN
OFF_KPB = OFF_KPA + LANES
IN_COLS = OFF_KPB + LANES
QAB_COLS = 2 * N_HEADS * LANES
KV_COLS = N_HEADS * LANES + N_HEADS * V_DIM

META_ROWS = 128
META_PAD = META_ROWS - N_META
ATTN_TQ = 256
ATTN_QGROUP = 1
FF_CHUNK = 256
DEC_PAGES = 32
DEC_SPLIT = 4


def _rms(x, g):
    return x * lax.rsqrt(jnp.mean(x * x, axis=-1, keepdims=True) + RMS_EPS) * g


def _dot(a, b):
    return jnp.dot(a, b, preferred_element_type=F32)


def _dot_nt(a, b):
    return lax.dot_general(a, b, (((1,), (1,)), ((), ())), preferred_element_type=F32)


def _softplus(z):
    return jnp.maximum(z, 0.0) + jnp.log1p(jnp.exp(-jnp.abs(z)))


def _gelu_tanh(x):
    cdf = 0.5 * (1.0 + jnp.tanh(math.sqrt(2.0 / math.pi) * (x + 0.044715 * (x * x * x))))
    return x * cdf


def _const_spec(shape):
    zeros = (0,) * len(shape)
    return pl.BlockSpec(shape, lambda *_: zeros, pipeline_mode=pl.Buffered(1))


def _front(x_ref, tab_ref, nm_ref, win_ref, qn_ref, wuq_ref, kvn_ref, proj_s, qab_s):
    xn = _rms(x_ref[...], nm_ref[...]).astype(BF16)
    proj_s[...] = _dot(xn, win_ref[...])
    cqn = _rms(proj_s[:, OFF_CQ:OFF_CQ + Q_LORA], qn_ref[...]).astype(BF16)
    qab_s[...] = _dot(cqn, wuq_ref[...])
    ckv_n = _rms(proj_s[:, OFF_CKV:OFF_CKV + KV_LORA], kvn_ref[...])
    ck_t = tab_ref[:, 2 * LANES:3 * LANES]
    sk_t = tab_ref[:, 3 * LANES:4 * LANES]
    kpe = proj_s[:, OFF_KPA:OFF_KPA + LANES] * ck_t + proj_s[:, OFF_KPB:OFF_KPB + LANES] * sk_t
    return ckv_n, kpe


def _q_head(qab_s, tab_ref, h):
    cq_t = tab_ref[:, 0:LANES]
    sq_t = tab_ref[:, LANES:2 * LANES]
    lo = h * LANES
    hi = N_HEADS * LANES + h * LANES
    return qab_s[:, lo:lo + LANES] * cq_t + qab_s[:, hi:hi + LANES] * sq_t


def _lru_terms(xc, wg_ref, bg_ref, la_ref):
    gates = _dot(xc.astype(BF16), wg_ref[...]) + bg_ref[...]
    r = jax.nn.sigmoid(gates[:, 0:D_RNN])
    i = jax.nn.sigmoid(gates[:, D_RNN:2 * D_RNN])
    log_a = (-LRU_C * r) * _softplus(-la_ref[...])
    a = jnp.exp(log_a)
    mult = jnp.sqrt(-jnp.tanh(log_a) * (a * a + 1.0))
    return a, mult * (i * xc)


def _in_prompt_kernel(x_ref, tab_ref, h0_ref, c0_ref,
                      nm_ref, win_ref, qn_ref, wuq_ref, kvn_ref, wkv_ref,
                      cw_ref, cb_ref, wg_ref, bg_ref, la_ref,
                      ckv_ref, kpe_ref, q_ref, k_ref, v_ref, rnn_ref, hf_ref, cf_ref,
                      proj_s, qab_s, kv_s, ubuf, a_s, b_s, h_s, hc_s, *, tm, n_pad):
    t = pl.program_id(1)
    ckv_n, kpe = _front(x_ref, tab_ref, nm_ref, win_ref, qn_ref, wuq_ref, kvn_ref, proj_s, qab_s)
    ckv_ref[...] = ckv_n
    kpe_ref[...] = kpe
    kv_s[...] = _dot(ckv_n.astype(BF16), wkv_ref[...])
    for h in range(N_HEADS):
        sl = slice(h * LANES, (h + 1) * LANES)
        q_ref[:, sl] = _q_head(qab_s, tab_ref, h).astype(BF16)
        k_ref[:, sl] = (kv_s[:, sl] + kpe).astype(BF16)
    v_ref[...] = kv_s[:, N_HEADS * LANES:KV_COLS].astype(BF16)

    @pl.when(t == 0)
    def _():
        ubuf[0:SUBLANES, :] = c0_ref[...]
        hc_s[...] = h0_ref[...]

    @pl.when(t > 0)
    def _():
        ubuf[0:SUBLANES, :] = ubuf[tm:tm + SUBLANES, :]

    u = proj_s[:, OFF_U:OFF_U + D_RNN]
    ubuf[SUBLANES:SUBLANES + tm, :] = u
    cw = cw_ref[...]
    xc = cb_ref[...]
    for k in range(CONV_W - 1):
        lo = SUBLANES - (CONV_W - 1) + k
        xc = xc + ubuf[lo:lo + tm, :] * cw[k:k + 1, :]
    xc = xc + u * cw[CONV_W - 1:CONV_W, :]
    a, b = _lru_terms(xc, wg_ref, bg_ref, la_ref)
    row = lax.broadcasted_iota(jnp.int32, (tm, D_RNN), 0)
    if n_pad:
        b = jnp.where(row + t * tm >= n_pad, b, 0.0)

    sub = row & (SUBLANES - 1)
    for d in (1, 2, 4):
        keep = sub >= d
        b = jnp.where(keep, a * pltpu.roll(b, d, 0) + b, b)
        a = jnp.where(keep, a * pltpu.roll(a, d, 0), a)
    a_s[...] = a
    b_s[...] = b

    def group(gi, h_in):
        o = pl.multiple_of(gi * SUBLANES, SUBLANES)
        hr = a_s[pl.ds(o, SUBLANES), :] * h_in + b_s[pl.ds(o, SUBLANES), :]
        h_s[pl.ds(o, SUBLANES), :] = hr
        return hr[SUBLANES - 1:SUBLANES, :]

    h_last = lax.fori_loop(0, tm // SUBLANES, group, hc_s[...], unroll=8)
    hc_s[...] = h_last
    hf_ref[...] = h_last
    cf_ref[...] = ubuf[tm:tm + SUBLANES, :]
    rnn_ref[...] = (h_s[...] * _gelu_tanh(proj_s[:, OFF_G:OFF_G + D_RNN])).astype(BF16)


def _in_prompt(x, tab, h0, c0, w, *, tm, n_pad):
    nb, L, _ = x.shape
    nt = L // tm
    assert nt * tm == L
    bstate = (lambda b, t: (b, 0, 0)) if h0.shape[0] == nb else (lambda b, t: (0, 0, 0))
    row_spec = lambda c: pl.BlockSpec((None, tm, c), lambda b, t: (b, t, 0))
    in_specs = [
        row_spec(D_MODEL),
        pl.BlockSpec((tm, 4 * LANES), lambda b, t: (t, 0)),
        pl.BlockSpec((None, 1, D_RNN), bstate),
        pl.BlockSpec((None, SUBLANES, D_RNN), bstate),
        _const_spec((1, D_MODEL)), _const_spec((D_MODEL, IN_COLS)),
        _const_spec((1, Q_LORA)), _const_spec((Q_LORA, QAB_COLS)),
        _const_spec((1, KV_LORA)), _const_spec((KV_LORA, KV_COLS)),
        _const_spec((CONV_W, D_RNN)), _const_spec((1, D_RNN)),
        _const_spec((D_RNN, 2 * D_RNN)), _const_spec((1, 2 * D_RNN)), _const_spec((1, D_RNN)),
    ]
    out_shape = [
        jax.ShapeDtypeStruct((nb, L, KV_LORA), F32),
        jax.ShapeDtypeStruct((nb, L, LANES), F32),
        jax.ShapeDtypeStruct((nb, L, N_HEADS * LANES), BF16),
        jax.ShapeDtypeStruct((nb, L, N_HEADS * LANES), BF16),
        jax.ShapeDtypeStruct((nb, L, N_HEADS * V_DIM), BF16),
        jax.ShapeDtypeStruct((nb, L, D_RNN), BF16),
        jax.ShapeDtypeStruct((nb, 1, D_RNN), F32),
        jax.ShapeDtypeStruct((nb, SUBLANES, D_RNN), F32),
    ]
    out_specs = [
        row_spec(KV_LORA), row_spec(LANES), row_spec(N_HEADS * LANES), row_spec(N_HEADS * LANES),
        row_spec(N_HEADS * V_DIM), row_spec(D_RNN),
        pl.BlockSpec((None, 1, D_RNN), lambda b, t: (b, 0, 0)),
        pl.BlockSpec((None, SUBLANES, D_RNN), lambda b, t: (b, 0, 0)),
    ]
    scratch = [
        pltpu.VMEM((tm, IN_COLS), F32), pltpu.VMEM((tm, QAB_COLS), F32), pltpu.VMEM((tm, KV_COLS), F32),
        pltpu.VMEM((tm + SUBLANES, D_RNN), F32),
        pltpu.VMEM((tm, D_RNN), F32), pltpu.VMEM((tm, D_RNN), F32), pltpu.VMEM((tm, D_RNN), F32),
        pltpu.VMEM((1, D_RNN), F32),
    ]
    return pl.pallas_call(
        functools.partial(_in_prompt_kernel, tm=tm, n_pad=n_pad),
        grid=(nb, nt), in_specs=in_specs, out_specs=out_specs, out_shape=out_shape,
        scratch_shapes=scratch, name="in_prompt",
        compiler_params=pltpu.CompilerParams(
            dimension_semantics=("arbitrary", "arbitrary"), vmem_limit_bytes=VMEM_LIMIT),
    )(x, tab, h0, c0, w["norm_mix"], w["w_in"], w["q_norm"], w["w_uq"], w["kv_norm"], w["w_kv"],
      w["conv_w"], w["conv_b"], w["w_gates"], w["b_gates"], w["lru_a"])


def _in_sample_kernel(x_ref, tab_ref, h0_ref, cp_ref,
                      nm_ref, win_ref, qn_ref, wuq_ref, kvn_ref, wukt_ref,
                      cw_ref, cb_ref, wg_ref, bg_ref, la_ref,
                      ckv_ref, kpe_ref, qlat_ref, qrot_ref, rnn_ref, hf_ref, cf_ref,
                      proj_s, qab_s, xc_s, h_s, *, nb, nt):
    ckv_n, kpe = _front(x_ref, tab_ref, nm_ref, win_ref, qn_ref, wuq_ref, kvn_ref, proj_s, qab_s)
    ckv_ref[...] = ckv_n
    kpe_ref[...] = kpe
    for h in range(N_HEADS):
        qh = _q_head(qab_s, tab_ref, h).astype(BF16)
        qrot_ref[:, h * LANES:(h + 1) * LANES] = qh
        qlat_ref[h] = _dot(qh, wukt_ref[h]).astype(BF16)

    cw = cw_ref[...]

    def xpad(j):
        if j < CONV_W - 1:
            return cp_ref[j]
        jj = j - (CONV_W - 1)
        return proj_s[jj * nb:(jj + 1) * nb, OFF_U:OFF_U + D_RNN]

    for t in range(nt):
        xc = cb_ref[...]
        for k in range(CONV_W):
            xc = xc + xpad(t + k) * cw[k:k + 1, :]
        xc_s[t * nb:(t + 1) * nb, :] = xc
    a, b = _lru_terms(xc_s[...], wg_ref, bg_ref, la_ref)
    h = h0_ref[...]
    for t in range(nt):
        h = a[t * nb:(t + 1) * nb, :] * h + b[t * nb:(t + 1) * nb, :]
        h_s[t * nb:(t + 1) * nb, :] = h
    hf_ref[...] = h
    for j in range(CONV_W - 1):
        cf_ref[j] = xpad(nt + j)
    rnn_ref[...] = (h_s[...] * _gelu_tanh(proj_s[:, OFF_G:OFF_G + D_RNN])).astype(BF16)


def _in_sample(x, tab, h0, cp, w, *, nb, nt):
    n = nb * nt
    full = lambda shape: pl.BlockSpec(shape, lambda i: (0,) * len(shape))
    in_specs = [
        full((n, D_MODEL)), full((n, 4 * LANES)), full((nb, D_RNN)), full((CONV_W - 1, nb, D_RNN)),
        full((1, D_MODEL)), full((D_MODEL, IN_COLS)), full((1, Q_LORA)), full((Q_LORA, QAB_COLS)),
        full((1, KV_LORA)), full((N_HEADS, LANES, KV_LORA)),
        full((CONV_W, D_RNN)), full((1, D_RNN)), full((D_RNN, 2 * D_RNN)), full((1, 2 * D_RNN)),
        full((1, D_RNN)),
    ]
    out_shape = [
        jax.ShapeDtypeStruct((n, KV_LORA), F32),
        jax.ShapeDtypeStruct((n, LANES), F32),
        jax.ShapeDtypeStruct((N_HEADS, n, KV_LORA), BF16),
        jax.ShapeDtypeStruct((n, N_HEADS * LANES), BF16),
        jax.ShapeDtypeStruct((n, D_RNN), BF16),
        jax.ShapeDtypeStruct((nb, D_RNN), F32),
        jax.ShapeDtypeStruct((CONV_W - 1, nb, D_RNN), F32),
    ]
    out_specs = [full(s.shape) for s in out_shape]
    scratch = [
        pltpu.VMEM((n, IN_COLS), F32), pltpu.VMEM((n, QAB_COLS), F32),
        pltpu.VMEM((n, D_RNN), F32), pltpu.VMEM((n, D_RNN), F32),
    ]
    return pl.pallas_call(
        functools.partial(_in_sample_kernel, nb=nb, nt=nt),
        grid=(1,), in_specs=in_specs, out_specs=out_specs, out_shape=out_shape,
        scratch_shapes=scratch, name="in_sample",
        compiler_params=pltpu.CompilerParams(
            dimension_semantics=("arbitrary",), vmem_limit_bytes=VMEM_LIMIT),
    )(x, tab, h0, cp, w["norm_mix"], w["w_in"], w["q_norm"], w["w_uq"], w["kv_norm"], w["w_ukt"],
      w["conv_w"], w["conv_b"], w["w_gates"], w["b_gates"], w["lru_a"])


def _softmax_init(s, v):
    m = jnp.max(s, axis=-1, keepdims=True)
    p = jnp.exp(s - m)
    return m, jnp.sum(p, axis=-1, keepdims=True), _dot(p.astype(BF16), v)


def _merge_pair(st0, st1):
    o0 = st0[2] / st0[1]
    o1 = st1[2] / st1[1]
    lane = lax.broadcasted_iota(jnp.int32, o0.shape, 1)
    return jnp.where(lane < V_DIM, o0, o1).astype(BF16)


def _attn_meta_kernel(q_ref, k_ref, v_ref, o_ref, *, n_pad):
    n = q_ref.shape[0]
    row = lax.broadcasted_iota(jnp.int32, (n, n), 0)
    col = lax.broadcasted_iota(jnp.int32, (n, n), 1)
    mask = (col <= row) & ((col >= n_pad) | (row < n_pad))
    v = v_ref[...]
    sts = []
    for hh in range(2):
        sl = slice(hh * LANES, (hh + 1) * LANES)
        s = jnp.where(mask, _dot_nt(q_ref[:, sl], k_ref[:, sl]), -jnp.inf)
        sts.append(_softmax_init(s, v))
    o_ref[...] = _merge_pair(*sts)


def _attn_meta(q, k, v, *, n_pad):
    n = q.shape[0]
    return pl.pallas_call(
        functools.partial(_attn_meta_kernel, n_pad=n_pad),
        grid=(N_HEADS // 2,),
        in_specs=[pl.BlockSpec((n, 2 * LANES), lambda p: (0, p)),
                  pl.BlockSpec((n, 2 * LANES), lambda p: (0, p)),
                  pl.BlockSpec((n, LANES), lambda p: (0, p))],
        out_specs=pl.BlockSpec((n, LANES), lambda p: (0, p)),
        out_shape=jax.ShapeDtypeStruct((n, N_HEADS * V_DIM), BF16),
        name="attn_meta",
        compiler_params=pltpu.CompilerParams(dimension_semantics=("arbitrary",)),
    )(q, k, v)


def _attn_main_kernel(q_ref, k_ref, v_ref, kp_ref, vp_ref, o_ref, *, tq, n_pad):
    L = q_ref.shape[0]
    npre = kp_ref.shape[0]
    pre_ok = lax.broadcasted_iota(jnp.int32, (tq, npre), 1) >= n_pad
    causal = (lax.broadcasted_iota(jnp.int32, (tq, tq), 1)
              <= lax.broadcasted_iota(jnp.int32, (tq, tq), 0))
    rowmax = lambda s: jnp.max(s, axis=-1, keepdims=True)
    rowsum = lambda p: jnp.sum(p, axis=-1, keepdims=True)
    vp = vp_ref[...]
    nq = L // tq
    for qb0 in range(0, nq, ATTN_QGROUP):
        chains = [(qb, hh) for qb in range(qb0, min(qb0 + ATTN_QGROUP, nq)) for hh in range(2)]
        scores = []
        for qb, hh in chains:
            r0, n = qb * tq, (qb + 1) * tq
            hs = slice(hh * LANES, (hh + 1) * LANES)
            q = q_ref[r0:n, hs]
            sp = jnp.where(pre_ok, _dot_nt(q, kp_ref[:, hs]), -jnp.inf)
            sd = jnp.where(causal, _dot_nt(q, k_ref[r0:n, hs]), -jnp.inf)
            sm = _dot_nt(q, k_ref[0:r0, hs]) if qb else None
            scores.append((sp, sd, sm))
        probs = []
        for sp, sd, sm in scores:
            m = jnp.maximum(rowmax(sp), rowmax(sd))
            if sm is not None:
                m = jnp.maximum(m, rowmax(sm))
            pp = jnp.exp(sp - m)
            pd = jnp.exp(sd - m)
            l = rowsum(pp) + rowsum(pd)
            pm = None
            if sm is not None:
                pm = jnp.exp(sm - m)
                l = l + rowsum(pm)
                pm = pm.astype(BF16)
            probs.append((m, l, pp.astype(BF16), pd.astype(BF16), pm))
        sts = []
        for (qb, hh), (m, l, pp, pd, pm) in zip(chains, probs):
            r0, n = qb * tq, (qb + 1) * tq
            acc = _dot(pp, vp) + _dot(pd, v_ref[r0:n, :])
            if pm is not None:
                acc = acc + _dot(pm, v_ref[0:r0, :])
            sts.append((m, l, acc))
        for i in range(0, len(chains), 2):
            r0 = chains[i][0] * tq
            o_ref[r0:r0 + tq, :] = _merge_pair(sts[i], sts[i + 1])


def _attn_main(q, k, v, kpre, vpre, *, n_pad):
    B, L, _ = q.shape
    npre = kpre.shape[0]
    tq = min(ATTN_TQ, L)
    assert L % tq == 0
    return pl.pallas_call(
        functools.partial(_attn_main_kernel, tq=tq, n_pad=n_pad),
        grid=(B, N_HEADS // 2),
        in_specs=[pl.BlockSpec((None, L, 2 * LANES), lambda b, p: (b, 0, p)),
                  pl.BlockSpec((None, L, 2 * LANES), lambda b, p: (b, 0, p)),
                  pl.BlockSpec((None, L, LANES), lambda b, p: (b, 0, p)),
                  pl.BlockSpec((npre, 2 * LANES), lambda b, p: (0, p)),
                  pl.BlockSpec((npre, LANES), lambda b, p: (0, p))],
        out_specs=pl.BlockSpec((None, L, LANES), lambda b, p: (b, 0, p)),
        out_shape=jax.ShapeDtypeStruct((B, L, N_HEADS * V_DIM), BF16),
        name="attn_main",
        compiler_params=pltpu.CompilerParams(
            dimension_semantics=("arbitrary", "arbitrary"), vmem_limit_bytes=VMEM_LIMIT),
    )(q, k, v, kpre, vpre)


def _decode_kernel(pt_ref, ql_ref, qp_ref, cn_ref, kn_ref, ckv_hbm, kpe_hbm, o_ref,
                   cbuf, kbuf, ncbuf, nkbuf, sem, m_s, l_s, acc_s, *, layer, G, NC, T):
    b = pl.program_id(0)
    c = pl.program_id(1)
    n = b * NC + c
    total = pl.num_programs(0) * NC
    slot = lax.rem(n, 2)

    def copies(bb, cc, sl):
        out = []
        for g in range(G):
            page = pt_ref[bb, cc * G + g]
            out.append(pltpu.make_async_copy(ckv_hbm.at[layer, page], cbuf.at[sl, g], sem.at[0, sl]))
            out.append(pltpu.make_async_copy(kpe_hbm.at[layer, page], kbuf.at[sl, g], sem.at[1, sl]))
        return out

    def start_all(cps):
        for i, cp in enumerate(cps):
            cp.start(priority=(i // 2) % 2)

    @pl.when(n == 0)
    def _():
        ncbuf[...] = jnp.zeros_like(ncbuf)
        nkbuf[...] = jnp.zeros_like(nkbuf)
        start_all(copies(b, c, slot))

    @pl.when(n + 1 < total)
    def _():
        n1 = n + 1
        start_all(copies(n1 // NC, lax.rem(n1, NC), 1 - slot))

    pltpu.make_async_copy(ckv_hbm.at[layer, pl.ds(0, G)], cbuf.at[slot], sem.at[0, slot]).wait()
    pltpu.make_async_copy(kpe_hbm.at[layer, pl.ds(0, G)], kbuf.at[slot], sem.at[1, slot]).wait()

    ql = ql_ref[...].astype(F32)
    qp = qp_ref[...].astype(F32)

    @pl.when(c == 0)
    def _():
        m_s[...] = jnp.full_like(m_s, -jnp.inf)
        l_s[...] = jnp.zeros_like(l_s)
        acc_s[...] = jnp.zeros_like(acc_s)

    def partial_softmax(kvs, kpts, mask_fn=None):
        ss = [_dot_nt(ql, kv) + _dot(qp, kpt) for kv, kpt in zip(kvs, kpts)]
        if mask_fn is not None:
            ss = [mask_fn(s) for s in ss]
        ms = [jnp.max(s, axis=-1, keepdims=True) for s in ss]
        ps = [jnp.exp(s - m) for s, m in zip(ss, ms)]
        ls = [jnp.sum(p, axis=-1, keepdims=True) for p in ps]
        accs = [_dot(p, kv) for p, kv in zip(ps, kvs)]
        return list(zip(ms, ls, accs))

    def merge(parts):
        m_old = m_s[...]
        m_new = m_old
        for m, _, _ in parts:
            m_new = jnp.maximum(m_new, m)
        w_old = jnp.exp(m_old - m_new)
        l = w_old * l_s[...]
        acc = w_old * acc_s[...]
        for m, li, ai in parts:
            wi = jnp.exp(m - m_new)
            l = l + wi * li
            acc = acc + wi * ai
        return m_new, l, acc

    pp = G // DEC_SPLIT
    kvs = [cbuf[slot, sc * pp:(sc + 1) * pp].reshape(pp * PAGE_SIZE, KV_LORA) for sc in range(DEC_SPLIT)]
    kpts = [jnp.concatenate([kbuf[slot, sc * pp + g] for g in range(pp)], axis=1)
            for sc in range(DEC_SPLIT)]
    m_s[...], l_s[...], acc_s[...] = merge(partial_softmax(kvs, kpts))

    @pl.when(c == NC - 1)
    def _():
        ncbuf[0:T, :] = cn_ref[...]
        nkbuf[:, 0:T] = kn_ref[...]

        def mask(s):
            tq = lax.rem(lax.broadcasted_iota(jnp.int32, s.shape, 0), T)
            col = lax.broadcasted_iota(jnp.int32, s.shape, 1)
            return jnp.where(col <= tq, s, -jnp.inf)

        m, l, acc = merge(partial_softmax([ncbuf[...]], [nkbuf[...]], mask))
        o_ref[...] = acc / l


def _decode_attn(page_table, qlat, qpe, ckv_new, kpe_new_t, cache_ckv, cache_kpe_t, *, layer):
    DB, rows, _ = qlat.shape
    T = ckv_new.shape[1]
    n_pages = page_table.shape[1]
    G = min(DEC_PAGES, n_pages)
    assert n_pages % G == 0 and G % DEC_SPLIT == 0
    NC = n_pages // G
    grid_spec = pltpu.PrefetchScalarGridSpec(
        num_scalar_prefetch=1,
        grid=(DB, NC),
        in_specs=[
            pl.BlockSpec((None, rows, KV_LORA), lambda b, c, pt: (b, 0, 0)),
            pl.BlockSpec((None, rows, QK_ROPE), lambda b, c, pt: (b, 0, 0)),
            pl.BlockSpec((None, T, KV_LORA), lambda b, c, pt: (b, 0, 0)),
            pl.BlockSpec((None, QK_ROPE, T), lambda b, c, pt: (b, 0, 0)),
            pl.BlockSpec(memory_space=pl.ANY),
            pl.BlockSpec(memory_space=pl.ANY),
        ],
        out_specs=pl.BlockSpec((None, rows, KV_LORA), lambda b, c, pt: (b, 0, 0)),
        scratch_shapes=[
            pltpu.VMEM((2, G, PAGE_SIZE, KV_LORA), F32),
            pltpu.VMEM((2, G, QK_ROPE, PAGE_SIZE), F32),
            pltpu.VMEM((PAGE_SIZE, KV_LORA), F32),
            pltpu.VMEM((QK_ROPE, PAGE_SIZE), F32),
            pltpu.SemaphoreType.DMA((2, 2)),
            pltpu.VMEM((rows, 1), F32), pltpu.VMEM((rows, 1), F32), pltpu.VMEM((rows, KV_LORA), F32),
        ],
    )
    return pl.pallas_call(
        functools.partial(_decode_kernel, layer=layer, G=G, NC=NC, T=T),
        grid_spec=grid_spec,
        out_shape=jax.ShapeDtypeStruct((DB, rows, KV_LORA), F32),
        name="decode_attn",
        compiler_params=pltpu.CompilerParams(
            dimension_semantics=("arbitrary", "arbitrary"), vmem_limit_bytes=VMEM_LIMIT),
    )(page_table, qlat, qpe, ckv_new, kpe_new_t, cache_ckv, cache_kpe_t)


def _ffn_tail(x1, nf_ref, wg_ref, wu_ref, wd_ref, fn_ref, o_ref, acc_s, final):
    xn = _rms(x1, nf_ref[...]).astype(BF16)
    for ci in range(D_FF // FF_CHUNK):
        sl = slice(ci * FF_CHUNK, (ci + 1) * FF_CHUNK)
        hg = _dot(xn, wg_ref[:, sl])
        hu = _dot(xn, wu_ref[:, sl])
        act = ((hg * jax.nn.sigmoid(hg)) * hu).astype(BF16)
        part = _dot(act, wd_ref[sl, :])
        if ci == 0:
            acc_s[...] = part
        else:
            acc_s[...] += part
    x2 = x1 + acc_s[...]
    if final:
        x2 = _rms(x2, fn_ref[...])
    o_ref[...] = x2


def _out_kernel(x_ref, a_ref, r_ref, wo_ref, nf_ref, wg_ref, wu_ref, wd_ref, fn_ref, o_ref, acc_s,
                *, final):
    na = a_ref.shape[1]
    y = _dot(a_ref[...], wo_ref[0:na, :]) + _dot(r_ref[...], wo_ref[na:, :])
    _ffn_tail(x_ref[...] + y, nf_ref, wg_ref, wu_ref, wd_ref, fn_ref, o_ref, acc_s, final)


def _out_latent_kernel(x_ref, ol_ref, wuv_ref, r_ref, wo_ref, nf_ref, wg_ref, wu_ref, wd_ref, fn_ref,
                       o_ref, acc_s, *, final):
    na = N_HEADS * V_DIM
    y = _dot(r_ref[...], wo_ref[na:, :])
    for h in range(N_HEADS):
        ah = _dot(ol_ref[h], wuv_ref[h]).astype(BF16)
        y = y + _dot(ah, wo_ref[h * V_DIM:(h + 1) * V_DIM, :])
    _ffn_tail(x_ref[...] + y, nf_ref, wg_ref, wu_ref, wd_ref, fn_ref, o_ref, acc_s, final)


def _out_proj(x, attn, rnn, w, final_norm, *, tm, final, latent=False):
    n = x.shape[0]
    assert n % tm == 0
    row = lambda c: pl.BlockSpec((tm, c), lambda i: (i, 0))
    wspecs = [_const_spec((D_MODEL, D_MODEL)), _const_spec((1, D_MODEL)),
              _const_spec((D_MODEL, D_FF)), _const_spec((D_MODEL, D_FF)), _const_spec((D_FF, D_MODEL)),
              _const_spec((1, D_MODEL))]
    wargs = (w["w_out"], w["norm_ffn"], w["w_gate"], w["w_up"], w["w_down"], final_norm)
    if latent:
        kern = functools.partial(_out_latent_kernel, final=final)
        in_specs = [row(D_MODEL), pl.BlockSpec((N_HEADS, tm, KV_LORA), lambda i: (0, i, 0)),
                    _const_spec((N_HEADS, KV_LORA, V_DIM)), row(D_RNN)] + wspecs
        args = (x, attn, w["w_uv3"], rnn) + wargs
    else:
        kern = functools.partial(_out_kernel, final=final)
        in_specs = [row(D_MODEL), row(N_HEADS * V_DIM), row(D_RNN)] + wspecs
        args = (x, attn, rnn) + wargs
    return pl.pallas_call(
        kern, grid=(n // tm,), in_specs=in_specs, out_specs=row(D_MODEL),
        out_shape=jax.ShapeDtypeStruct((n, D_MODEL), F32),
        scratch_shapes=[pltpu.VMEM((tm, D_MODEL), F32)],
        name="out_latent" if latent else "out_proj",
        compiler_params=pltpu.CompilerParams(
            dimension_semantics=("arbitrary",), vmem_limit_bytes=VMEM_LIMIT),
    )(*args)


def _rot_half_cols(w):
    half = QK_ROPE // 2
    return jnp.concatenate([-w[..., half:], w[..., :half]], axis=-1)


def _prep_layer(l, p):
    w_in = p["w_in"][l]
    d = w_in.shape[0]
    o_kpe = Q_LORA + KV_LORA
    kpe = w_in[:, o_kpe:o_kpe + QK_ROPE]
    z = lambda c: jnp.zeros((d, c), F32)
    w_in_r = jnp.concatenate(
        [w_in[:, :o_kpe], w_in[:, o_kpe + QK_ROPE:],
         z(QK_NOPE), kpe, z(LANES - QK_NOPE - QK_ROPE),
         z(QK_NOPE), _rot_half_cols(kpe), z(LANES - QK_NOPE - QK_ROPE)], axis=1)

    w_uq = p["w_uq"][l].reshape(Q_LORA, N_HEADS, QK_NOPE + QK_ROPE)
    nope, pe = w_uq[..., :QK_NOPE], w_uq[..., QK_NOPE:]
    zq = lambda c: jnp.zeros((Q_LORA, N_HEADS, c), F32)
    qa = jnp.concatenate([nope, pe, zq(LANES - QK_NOPE - QK_ROPE)], axis=-1)
    qb = jnp.concatenate([zq(QK_NOPE), _rot_half_cols(pe), zq(LANES - QK_NOPE - QK_ROPE)], axis=-1)
    w_uq_r = jnp.concatenate([qa.reshape(Q_LORA, -1), qb.reshape(Q_LORA, -1)], axis=1)

    w_uk = p["w_uk"][l]
    w_uv = p["w_uv"][l]
    w_k = jnp.concatenate([w_uk, jnp.zeros((KV_LORA, N_HEADS, LANES - QK_NOPE), F32)], axis=-1)
    w_kv = jnp.concatenate([w_k.reshape(KV_LORA, -1), w_uv.reshape(KV_LORA, -1)], axis=1)
    w_ukt = jnp.concatenate(
        [jnp.transpose(w_uk, (1, 2, 0)), jnp.zeros((N_HEADS, LANES - QK_NOPE, KV_LORA), F32)], axis=1)

    eye = jnp.eye(N_RNN_BLOCKS, dtype=F32)
    bd = lambda wb: jnp.einsum("nde,nm->ndme", wb, eye).reshape(D_RNN, D_RNN)
    w_gates = jnp.concatenate([bd(p["w_ra"][l]), bd(p["w_ri"][l])], axis=1)
    row = lambda v: v.reshape(1, -1).astype(F32)
    return {
        "norm_mix": row(p["norm_mix"][l]), "w_in": w_in_r.astype(BF16),
        "q_norm": row(p["q_norm"][l]), "w_uq": w_uq_r.astype(BF16),
        "kv_norm": row(p["kv_norm"][l]), "w_kv": w_kv.astype(BF16), "w_ukt": w_ukt.astype(BF16),
        "w_uv3": jnp.transpose(w_uv, (1, 0, 2)).astype(BF16),
        "conv_w": p["conv_w"][l].astype(F32), "conv_b": row(p["conv_b"][l]),
        "w_gates": w_gates.astype(BF16),
        "b_gates": jnp.concatenate([p["b_ra"][l], p["b_ri"][l]]).reshape(1, -1).astype(F32),
        "lru_a": row(p["lru_a"][l]),
        "w_out": p["w_out"][l].astype(BF16), "norm_ffn": row(p["norm_ffn"][l]),
        "w_gate": p["w_gate"][l].astype(BF16), "w_up": p["w_up"][l].astype(BF16),
        "w_down": p["w_down"][l].astype(BF16),
    }


def _rope_table(pos):
    half = QK_ROPE // 2
    inv = ROPE_THETA ** (-jnp.arange(half, dtype=F32) / half)
    ang = pos[:, None] * inv[None, :]
    cos = jnp.tile(jnp.cos(ang), (1, 2))
    sin = jnp.tile(jnp.sin(ang), (1, 2))
    n = pos.shape[0]
    one = jnp.ones((n, QK_NOPE), F32)
    z_lo = jnp.zeros((n, QK_NOPE), F32)
    z_hi = jnp.zeros((n, LANES - QK_NOPE - QK_ROPE), F32)
    return jnp.concatenate(
        [one * ATTN_SCALE, cos * ATTN_SCALE, z_hi, z_lo, sin * ATTN_SCALE, z_hi,
         z_lo, cos, z_hi, z_lo, sin, z_hi], axis=1)


def _pick_tile(n, pref):
    t = min(pref, n)
    while n % t:
        t //= 2
    return t


def kernel(x_prompt, x_sample, cache_ckv, cache_kpe, state_h, state_conv, page_table, meta_tokens,
           norm_mix, w_in, q_norm, w_uq, kv_norm, w_uk, w_uv, conv_w, conv_b, w_ra, b_ra, w_ri, b_ri,
           lru_a, w_out, norm_ffn, w_gate, w_up, w_down, final_norm):
    params = dict(norm_mix=norm_mix, w_in=w_in, q_norm=q_norm, w_uq=w_uq, kv_norm=kv_norm, w_uk=w_uk,
                  w_uv=w_uv, conv_w=conv_w, conv_b=conv_b, w_ra=w_ra, b_ra=b_ra, w_ri=w_ri, b_ri=b_ri,
                  lru_a=lru_a, w_out=w_out, norm_ffn=norm_ffn, w_gate=w_gate, w_up=w_up, w_down=w_down)
    depth = w_in.shape[0]
    B, S, _ = x_prompt.shape
    DB, T, _ = x_sample.shape
    assert T >= CONV_W - 1
    past_len = page_table.shape[1] * PAGE_SIZE
    fnorm = final_norm.reshape(1, -1).astype(F32)

    tab_meta = _rope_table(jnp.arange(META_ROWS, dtype=F32) - META_PAD)
    tab_main = _rope_table(N_META + jnp.arange(S, dtype=F32))
    tab_samp = _rope_table(jnp.repeat(past_len + jnp.arange(T, dtype=F32), DB))

    xm = jnp.concatenate([jnp.zeros((META_PAD, D_MODEL), F32), meta_tokens.astype(F32)], axis=0)[None]
    xp = x_prompt
    xs = jnp.transpose(x_sample, (1, 0, 2)).reshape(T * DB, D_MODEL)
    zero_h = jnp.zeros((1, 1, D_RNN), F32)
    zero_c = jnp.zeros((1, SUBLANES, D_RNN), F32)
    tm_in = _pick_tile(S, 512)
    tm_out = _pick_tile(B * S, 512)
    tm_s = _pick_tile(T * DB, 512)

    cache_kpe_t = jnp.transpose(cache_kpe, (0, 1, 3, 2))
    outs = {k: [] for k in ("ckv_p", "kpe_p", "h_p", "cv_p", "ckv_s", "kpe_s", "h_s", "cv_s")}
    rope_lanes = slice(QK_NOPE, QK_NOPE + QK_ROPE)
    for l in range(depth):
        w = _prep_layer(l, params)
        last = l == depth - 1

        ckv_m, kpe_m, q_m, k_m, v_m, rnn_m, h_m, c_m = _in_prompt(
            xm, tab_meta, zero_h, zero_c, w, tm=META_ROWS, n_pad=META_PAD)
        attn_m = _attn_meta(q_m[0], k_m[0], v_m[0], n_pad=META_PAD)
        xm = _out_proj(xm[0], attn_m, rnn_m[0], w, fnorm, tm=META_ROWS, final=False)[None]

        ckv, kpe, q, k, v, rnn, h_f, c_f = _in_prompt(xp, tab_main, h_m, c_m, w, tm=tm_in, n_pad=0)
        attn = _attn_main(q, k, v, k_m[0], v_m[0], n_pad=META_PAD)
        xp = _out_proj(xp.reshape(B * S, D_MODEL), attn.reshape(B * S, -1), rnn.reshape(B * S, -1),
                       w, fnorm, tm=tm_out, final=last).reshape(B, S, D_MODEL)
        bc = lambda a: jnp.broadcast_to(a[None], (B,) + a.shape)
        outs["ckv_p"].append(jnp.concatenate([bc(ckv_m[0, META_PAD:]), ckv], axis=1))
        outs["kpe_p"].append(jnp.concatenate(
            [bc(kpe_m[0, META_PAD:, rope_lanes]), kpe[..., rope_lanes]], axis=1))
        outs["h_p"].append(h_f[:, 0])
        outs["cv_p"].append(c_f[:, SUBLANES - (CONV_W - 1):])

        cp = jnp.transpose(state_conv[l], (1, 0, 2))
        ckv_s, kpe_s, qlat, qrot, rnn_s, h_s, cv_s = _in_sample(
            xs, tab_samp, state_h[l], cp, w, nb=DB, nt=T)
        ckv_new = jnp.transpose(ckv_s.reshape(T, DB, KV_LORA), (1, 0, 2))
        kpe_new = jnp.transpose(kpe_s[:, rope_lanes].reshape(T, DB, QK_ROPE), (1, 0, 2))
        ql = jnp.transpose(qlat.reshape(N_HEADS, T, DB, KV_LORA), (2, 0, 1, 3)).reshape(DB, N_HEADS * T, KV_LORA)
        qp = qrot.reshape(T, DB, N_HEADS, LANES)[..., rope_lanes]
        qp = jnp.transpose(qp, (1, 2, 0, 3)).reshape(DB, N_HEADS * T, QK_ROPE)
        o_lat = _decode_attn(page_table, ql, qp, ckv_new, jnp.transpose(kpe_new, (0, 2, 1)),
                             cache_ckv, cache_kpe_t, layer=l)
        ol = jnp.transpose(o_lat.reshape(DB, N_HEADS, T, KV_LORA), (1, 2, 0, 3))
        ol = ol.reshape(N_HEADS, T * DB, KV_LORA).astype(BF16)
        xs = _out_proj(xs, ol, rnn_s, w, fnorm, tm=tm_s, final=last, latent=True)
        outs["ckv_s"].append(ckv_new)
        outs["kpe_s"].append(kpe_new)
        outs["h_s"].append(h_s)
        outs["cv_s"].append(jnp.transpose(cv_s, (1, 0, 2)))

    y_sample = jnp.transpose(xs.reshape(T, DB, D_MODEL), (1, 0, 2))
    st = lambda k: jnp.stack(outs[k])
    return (xp, y_sample, st("ckv_p"), st("kpe_p"), st("h_p"), st("cv_p"),
            st("ckv_s"), st("kpe_s"), st("h_s"), st("cv_s"))
```

```python
import functools
import math

import jax
import jax.numpy as jnp
from jax import lax
from jax.experimental import pallas as pl
from jax.experimental.pallas import tpu as pltpu

F32 = jnp.float32
BF16 = jnp.bfloat16

D_MODEL = 1024
N_META = 16
N_HEADS = 8
QK_NOPE = 64
QK_ROPE = 32
V_DIM = 64
Q_LORA = 384
KV_LORA = 256
ROPE_THETA = 10000.0
D_RNN = 512
N_RNN_BLOCKS = 8
RNN_BLOCK = D_RNN // N_RNN_BLOCKS
CONV_W = 4
LRU_C = 8.0
D_FF = 2816
RMS_EPS = 1e-6
PAGE_SIZE = 128
ATTN_SCALE = 1.0 / math.sqrt(QK_NOPE + QK_ROPE)

LANES = 128
SUBLANES = 8
VMEM_LIMIT = 56 * 1024 * 1024

OFF_CQ = 0
OFF_CKV = OFF_CQ + Q_LORA
OFF_U = OFF_CKV + KV_LORA
OFF_G = OFF_U + D_RNN
OFF_KPA = OFF_G + D_RNN
OFF_KPB = OFF_KPA + LANES
IN_COLS = OFF_KPB + LANES
QAB_COLS = 2 * N_HEADS * LANES
KV_COLS = N_HEADS * LANES + N_HEADS * V_DIM

META_ROWS = 128
META_PAD = META_ROWS - N_META
ATTN_TQ = 256
ATTN_QGROUP = 1
FF_CHUNK = 256
DEC_PAGES = 64
DEC_SPLIT = 8


def _rms(x, g):
    return x * lax.rsqrt(jnp.mean(x * x, axis=-1, keepdims=True) + RMS_EPS) * g


def _dot(a, b):
    return jnp.dot(a, b, preferred_element_type=F32)


def _dot_nt(a, b):
    return lax.dot_general(a, b, (((1,), (1,)), ((), ())), preferred_element_type=F32)


def _softplus(z):
    return jnp.maximum(z, 0.0) + jnp.log1p(jnp.exp(-jnp.abs(z)))


def _gelu_tanh(x):
    cdf = 0.5 * (1.0 + jnp.tanh(math.sqrt(2.0 / math.pi) * (x + 0.044715 * (x * x * x))))
    return x * cdf


def _const_spec(shape):
    zeros = (0,) * len(shape)
    return pl.BlockSpec(shape, lambda *_: zeros, pipeline_mode=pl.Buffered(1))


def _front(x_ref, tab_ref, nm_ref, win_ref, qn_ref, wuq_ref, kvn_ref, proj_s, qab_s):
    xn = _rms(x_ref[...], nm_ref[...]).astype(BF16)
    proj_s[...] = _dot(xn, win_ref[...])
    cqn = _rms(proj_s[:, OFF_CQ:OFF_CQ + Q_LORA], qn_ref[...]).astype(BF16)
    qab_s[...] = _dot(cqn, wuq_ref[...])
    ckv_n = _rms(proj_s[:, OFF_CKV:OFF_CKV + KV_LORA], kvn_ref[...])
    ck_t = tab_ref[:, 2 * LANES:3 * LANES]
    sk_t = tab_ref[:, 3 * LANES:4 * LANES]
    kpe = proj_s[:, OFF_KPA:OFF_KPA + LANES] * ck_t + proj_s[:, OFF_KPB:OFF_KPB + LANES] * sk_t
    return ckv_n, kpe


def _q_head(qab_s, tab_ref, h):
    cq_t = tab_ref[:, 0:LANES]
    sq_t = tab_ref[:, LANES:2 * LANES]
    lo = h * LANES
    hi = N_HEADS * LANES + h * LANES
    return qab_s[:, lo:lo + LANES] * cq_t + qab_s[:, hi:hi + LANES] * sq_t


def _lru_terms(xc, wg_ref, bg_ref, la_ref):
    gates = _dot(xc.astype(BF16), wg_ref[...]) + bg_ref[...]
    r = jax.nn.sigmoid(gates[:, 0:D_RNN])
    i = jax.nn.sigmoid(gates[:, D_RNN:2 * D_RNN])
    log_a = (-LRU_C * r) * _softplus(-la_ref[...])
    a = jnp.exp(log_a)
    mult = jnp.sqrt(-jnp.tanh(log_a) * (a * a + 1.0))
    return a, mult * (i * xc)


def _in_prompt_kernel(x_ref, tab_ref, h0_ref, c0_ref,
                      nm_ref, win_ref, qn_ref, wuq_ref, kvn_ref, wkv_ref,
                      cw_ref, cb_ref, wg_ref, bg_ref, la_ref,
                      ckv_ref, kpe_ref, q_ref, k_ref, v_ref, rnn_ref, hf_ref, cf_ref,
                      proj_s, qab_s, kv_s, ubuf, a_s, b_s, h_s, hc_s, *, tm, n_pad):
    t = pl.program_id(1)
    ckv_n, kpe = _front(x_ref, tab_ref, nm_ref, win_ref, qn_ref, wuq_ref, kvn_ref, proj_s, qab_s)
    ckv_ref[...] = ckv_n
    kpe_ref[...] = kpe
    kv_s[...] = _dot(ckv_n.astype(BF16), wkv_ref[...])
    for h in range(N_HEADS):
        sl = slice(h * LANES, (h + 1) * LANES)
        q_ref[:, sl] = _q_head(qab_s, tab_ref, h).astype(BF16)
        k_ref[:, sl] = (kv_s[:, sl] + kpe).astype(BF16)
    v_ref[...] = kv_s[:, N_HEADS * LANES:KV_COLS].astype(BF16)

    @pl.when(t == 0)
    def _():
        ubuf[0:SUBLANES, :] = c0_ref[...]
        hc_s[...] = h0_ref[...]

    @pl.when(t > 0)
    def _():
        ubuf[0:SUBLANES, :] = ubuf[tm:tm + SUBLANES, :]

    u = proj_s[:, OFF_U:OFF_U + D_RNN]
    ubuf[SUBLANES:SUBLANES + tm, :] = u
    cw = cw_ref[...]
    xc = cb_ref[...]
    for k in range(CONV_W - 1):
        lo = SUBLANES - (CONV_W - 1) + k
        xc = xc + ubuf[lo:lo + tm, :] * cw[k:k + 1, :]
    xc = xc + u * cw[CONV_W - 1:CONV_W, :]
    a, b = _lru_terms(xc, wg_ref, bg_ref, la_ref)
    row = lax.broadcasted_iota(jnp.int32, (tm, D_RNN), 0)
    if n_pad:
        b = jnp.where(row + t * tm >= n_pad, b, 0.0)

    a = a.reshape(tm // SUBLANES, SUBLANES, D_RNN)
    b = b.reshape(tm // SUBLANES, SUBLANES, D_RNN)
    sub = lax.broadcasted_iota(jnp.int32, a.shape, 1)
    for d in (1, 2, 4):
        keep = sub >= d
        b = jnp.where(keep, a * pltpu.roll(b, d, 1) + b, b)
        a = jnp.where(keep, a * pltpu.roll(a, d, 1), a)
    a_s[...] = a.reshape(tm, D_RNN)
    b_s[...] = b.reshape(tm, D_RNN)

    def group(gi, h_in):
        o = pl.multiple_of(gi * SUBLANES, SUBLANES)
        hr = a_s[pl.ds(o, SUBLANES), :] * h_in + b_s[pl.ds(o, SUBLANES), :]
        h_s[pl.ds(o, SUBLANES), :] = hr
        return hr[SUBLANES - 1:SUBLANES, :]

    h_last = lax.fori_loop(0, tm // SUBLANES, group, hc_s[...], unroll=8)
    hc_s[...] = h_last
    hf_ref[...] = h_last
    cf_ref[...] = ubuf[tm:tm + SUBLANES, :]
    rnn_ref[...] = (h_s[...] * _gelu_tanh(proj_s[:, OFF_G:OFF_G + D_RNN])).astype(BF16)


def _in_prompt(x, tab, h0, c0, w, *, tm, n_pad):
    nb, L, _ = x.shape
    nt = L // tm
    assert nt * tm == L
    bstate = (lambda b, t: (b, 0, 0)) if h0.shape[0] == nb else (lambda b, t: (0, 0, 0))
    row_spec = lambda c: pl.BlockSpec((None, tm, c), lambda b, t: (b, t, 0))
    in_specs = [
        row_spec(D_MODEL),
        pl.BlockSpec((tm, 4 * LANES), lambda b, t: (t, 0)),
        pl.BlockSpec((None, 1, D_RNN), bstate),
        pl.BlockSpec((None, SUBLANES, D_RNN), bstate),
        _const_spec((1, D_MODEL)), _const_spec((D_MODEL, IN_COLS)),
        _const_spec((1, Q_LORA)), _const_spec((Q_LORA, QAB_COLS)),
        _const_spec((1, KV_LORA)), _const_spec((KV_LORA, KV_COLS)),
        _const_spec((CONV_W, D_RNN)), _const_spec((1, D_RNN)),
        _const_spec((D_RNN, 2 * D_RNN)), _const_spec((1, 2 * D_RNN)), _const_spec((1, D_RNN)),
    ]
    out_shape = [
        jax.ShapeDtypeStruct((nb, L, KV_LORA), F32),
        jax.ShapeDtypeStruct((nb, L, LANES), F32),
        jax.ShapeDtypeStruct((nb, L, N_HEADS * LANES), BF16),
        jax.ShapeDtypeStruct((nb, L, N_HEADS * LANES), BF16),
        jax.ShapeDtypeStruct((nb, L, N_HEADS * V_DIM), BF16),
        jax.ShapeDtypeStruct((nb, L, D_RNN), BF16),
        jax.ShapeDtypeStruct((nb, 1, D_RNN), F32),
        jax.ShapeDtypeStruct((nb, SUBLANES, D_RNN), F32),
    ]
    out_specs = [
        row_spec(KV_LORA), row_spec(LANES), row_spec(N_HEADS * LANES), row_spec(N_HEADS * LANES),
        row_spec(N_HEADS * V_DIM), row_spec(D_RNN),
        pl.BlockSpec((None, 1, D_RNN), lambda b, t: (b, 0, 0)),
        pl.BlockSpec((None, SUBLANES, D_RNN), lambda b, t: (b, 0, 0)),
    ]
    scratch = [
        pltpu.VMEM((tm, IN_COLS), F32), pltpu.VMEM((tm, QAB_COLS), F32), pltpu.VMEM((tm, KV_COLS), F32),
        pltpu.VMEM((tm + SUBLANES, D_RNN), F32),
        pltpu.VMEM((tm, D_RNN), F32), pltpu.VMEM((tm, D_RNN), F32), pltpu.VMEM((tm, D_RNN), F32),
        pltpu.VMEM((1, D_RNN), F32),
    ]
    return pl.pallas_call(
        functools.partial(_in_prompt_kernel, tm=tm, n_pad=n_pad),
        grid=(nb, nt), in_specs=in_specs, out_specs=out_specs, out_shape=out_shape,
        scratch_shapes=scratch, name="in_prompt",
        compiler_params=pltpu.CompilerParams(
            dimension_semantics=("arbitrary", "arbitrary"), vmem_limit_bytes=VMEM_LIMIT),
    )(x, tab, h0, c0, w["norm_mix"], w["w_in"], w["q_norm"], w["w_uq"], w["kv_norm"], w["w_kv"],
      w["conv_w"], w["conv_b"], w["w_gates"], w["b_gates"], w["lru_a"])


def _in_sample_kernel(x_ref, tab_ref, h0_ref, cp_ref,
                      nm_ref, win_ref, qn_ref, wuq_ref, kvn_ref, wukt_ref,
                      cw_ref, cb_ref, wg_ref, bg_ref, la_ref,
                      ckv_ref, kpe_ref, qlat_ref, qrot_ref, rnn_ref, hf_ref, cf_ref,
                      proj_s, qab_s, xc_s, h_s, *, nb, nt):
    ckv_n, kpe = _front(x_ref, tab_ref, nm_ref, win_ref, qn_ref, wuq_ref, kvn_ref, proj_s, qab_s)
    ckv_ref[...] = ckv_n
    kpe_ref[...] = kpe
    for h in range(N_HEADS):
        qh = _q_head(qab_s, tab_ref, h).astype(BF16)
        qrot_ref[:, h * LANES:(h + 1) * LANES] = qh
        qlat_ref[h] = _dot(qh, wukt_ref[h]).astype(BF16)

    cw = cw_ref[...]

    def xpad(j):
        if j < CONV_W - 1:
            return cp_ref[j]
        jj = j - (CONV_W - 1)
        return proj_s[jj * nb:(jj + 1) * nb, OFF_U:OFF_U + D_RNN]

    for t in range(nt):
        xc = cb_ref[...]
        for k in range(CONV_W):
            xc = xc + xpad(t + k) * cw[k:k + 1, :]
        xc_s[t * nb:(t + 1) * nb, :] = xc
    a, b = _lru_terms(xc_s[...], wg_ref, bg_ref, la_ref)
    h = h0_ref[...]
    for t in range(nt):
        h = a[t * nb:(t + 1) * nb, :] * h + b[t * nb:(t + 1) * nb, :]
        h_s[t * nb:(t + 1) * nb, :] = h
    hf_ref[...] = h
    for j in range(CONV_W - 1):
        cf_ref[j] = xpad(nt + j)
    rnn_ref[...] = (h_s[...] * _gelu_tanh(proj_s[:, OFF_G:OFF_G + D_RNN])).astype(BF16)


def _in_sample(x, tab, h0, cp, w, *, nb, nt):
    n = nb * nt
    full = lambda shape: pl.BlockSpec(shape, lambda i: (0,) * len(shape))
    in_specs = [
        full((n, D_MODEL)), full((n, 4 * LANES)), full((nb, D_RNN)), full((CONV_W - 1, nb, D_RNN)),
        full((1, D_MODEL)), full((D_MODEL, IN_COLS)), full((1, Q_LORA)), full((Q_LORA, QAB_COLS)),
        full((1, KV_LORA)), full((N_HEADS, LANES, KV_LORA)),
        full((CONV_W, D_RNN)), full((1, D_RNN)), full((D_RNN, 2 * D_RNN)), full((1, 2 * D_RNN)),
        full((1, D_RNN)),
    ]
    out_shape = [
        jax.ShapeDtypeStruct((n, KV_LORA), F32),
        jax.ShapeDtypeStruct((n, LANES), F32),
        jax.ShapeDtypeStruct((N_HEADS, n, KV_LORA), BF16),
        jax.ShapeDtypeStruct((n, N_HEADS * LANES), BF16),
        jax.ShapeDtypeStruct((n, D_RNN), BF16),
        jax.ShapeDtypeStruct((nb, D_RNN), F32),
        jax.ShapeDtypeStruct((CONV_W - 1, nb, D_RNN), F32),
    ]
    out_specs = [full(s.shape) for s in out_shape]
    scratch = [
        pltpu.VMEM((n, IN_COLS), F32), pltpu.VMEM((n, QAB_COLS), F32),
        pltpu.VMEM((n, D_RNN), F32), pltpu.VMEM((n, D_RNN), F32),
    ]
    return pl.pallas_call(
        functools.partial(_in_sample_kernel, nb=nb, nt=nt),
        grid=(1,), in_specs=in_specs, out_specs=out_specs, out_shape=out_shape,
        scratch_shapes=scratch, name="in_sample",
        compiler_params=pltpu.CompilerParams(
            dimension_semantics=("arbitrary",), vmem_limit_bytes=VMEM_LIMIT),
    )(x, tab, h0, cp, w["norm_mix"], w["w_in"], w["q_norm"], w["w_uq"], w["kv_norm"], w["w_ukt"],
      w["conv_w"], w["conv_b"], w["w_gates"], w["b_gates"], w["lru_a"])


def _softmax_init(s, v):
    m = jnp.max(s, axis=-1, keepdims=True)
    p = jnp.exp(s - m)
    return m, jnp.sum(p, axis=-1, keepdims=True), _dot(p.astype(BF16), v)


def _merge_pair(st0, st1):
    o0 = st0[2] / st0[1]
    o1 = st1[2] / st1[1]
    lane = lax.broadcasted_iota(jnp.int32, o0.shape, 1)
    return jnp.where(lane < V_DIM, o0, o1).astype(BF16)


def _attn_meta_kernel(q_ref, k_ref, v_ref, o_ref, *, n_pad):
    n = q_ref.shape[0]
    row = lax.broadcasted_iota(jnp.int32, (n, n), 0)
    col = lax.broadcasted_iota(jnp.int32, (n, n), 1)
    mask = (col <= row) & ((col >= n_pad) | (row < n_pad))
    v = v_ref[...]
    sts = []
    for hh in range(2):
        sl = slice(hh * LANES, (hh + 1) * LANES)
        s = jnp.where(mask, _dot_nt(q_ref[:, sl], k_ref[:, sl]), -jnp.inf)
        sts.append(_softmax_init(s, v))
    o_ref[...] = _merge_pair(*sts)


def _attn_meta(q, k, v, *, n_pad):
    n = q.shape[0]
    return pl.pallas_call(
        functools.partial(_attn_meta_kernel, n_pad=n_pad),
        grid=(N_HEADS // 2,),
        in_specs=[pl.BlockSpec((n, 2 * LANES), lambda p: (0, p)),
                  pl.BlockSpec((n, 2 * LANES), lambda p: (0, p)),
                  pl.BlockSpec((n, LANES), lambda p: (0, p))],
        out_specs=pl.BlockSpec((n, LANES), lambda p: (0, p)),
        out_shape=jax.ShapeDtypeStruct((n, N_HEADS * V_DIM), BF16),
        name="attn_meta",
        compiler_params=pltpu.CompilerParams(dimension_semantics=("arbitrary",)),
    )(q, k, v)


def _attn_main_kernel(q_ref, k_ref, v_ref, kp_ref, vp_ref, o_ref, *, tq, n_pad):
    L = q_ref.shape[0]
    npre = kp_ref.shape[0]
    pre_ok = lax.broadcasted_iota(jnp.int32, (tq, npre), 1) >= n_pad
    causal = (lax.broadcasted_iota(jnp.int32, (tq, tq), 1)
              <= lax.broadcasted_iota(jnp.int32, (tq, tq), 0))
    rowmax = lambda s: jnp.max(s, axis=-1, keepdims=True)
    rowsum = lambda p: jnp.sum(p, axis=-1, keepdims=True)
    vp = vp_ref[...]
    nq = L // tq
    for qb0 in range(0, nq, ATTN_QGROUP):
        chains = [(qb, hh) for qb in range(qb0, min(qb0 + ATTN_QGROUP, nq)) for hh in range(2)]
        scores = []
        for qb, hh in chains:
            r0, n = qb * tq, (qb + 1) * tq
            hs = slice(hh * LANES, (hh + 1) * LANES)
            q = q_ref[r0:n, hs]
            sp = jnp.where(pre_ok, _dot_nt(q, kp_ref[:, hs]), -jnp.inf)
            sd = jnp.where(causal, _dot_nt(q, k_ref[r0:n, hs]), -jnp.inf)
            sm = _dot_nt(q, k_ref[0:r0, hs]) if qb else None
            scores.append((sp, sd, sm))
        probs = []
        for sp, sd, sm in scores:
            m = jnp.maximum(rowmax(sp), rowmax(sd))
            if sm is not None:
                m = jnp.maximum(m, rowmax(sm))
            pp = jnp.exp(sp - m)
            pd = jnp.exp(sd - m)
            l = rowsum(pp) + rowsum(pd)
            pm = None
            if sm is not None:
                pm = jnp.exp(sm - m)
                l = l + rowsum(pm)
                pm = pm.astype(BF16)
            probs.append((m, l, pp.astype(BF16), pd.astype(BF16), pm))
        sts = []
        for (qb, hh), (m, l, pp, pd, pm) in zip(chains, probs):
            r0, n = qb * tq, (qb + 1) * tq
            acc = _dot(pp, vp) + _dot(pd, v_ref[r0:n, :])
            if pm is not None:
                acc = acc + _dot(pm, v_ref[0:r0, :])
            sts.append((m, l, acc))
        for i in range(0, len(chains), 2):
            r0 = chains[i][0] * tq
            o_ref[r0:r0 + tq, :] = _merge_pair(sts[i], sts[i + 1])


def _attn_main(q, k, v, kpre, vpre, *, n_pad):
    B, L, _ = q.shape
    npre = kpre.shape[0]
    tq = min(ATTN_TQ, L)
    assert L % tq == 0
    return pl.pallas_call(
        functools.partial(_attn_main_kernel, tq=tq, n_pad=n_pad),
        grid=(B, N_HEADS // 2),
        in_specs=[pl.BlockSpec((None, L, 2 * LANES), lambda b, p: (b, 0, p)),
                  pl.BlockSpec((None, L, 2 * LANES), lambda b, p: (b, 0, p)),
                  pl.BlockSpec((None, L, LANES), lambda b, p: (b, 0, p)),
                  pl.BlockSpec((npre, 2 * LANES), lambda b, p: (0, p)),
                  pl.BlockSpec((npre, LANES), lambda b, p: (0, p))],
        out_specs=pl.BlockSpec((None, L, LANES), lambda b, p: (b, 0, p)),
        out_shape=jax.ShapeDtypeStruct((B, L, N_HEADS * V_DIM), BF16),
        name="attn_main",
        compiler_params=pltpu.CompilerParams(
            dimension_semantics=("arbitrary", "arbitrary"), vmem_limit_bytes=VMEM_LIMIT),
    )(q, k, v, kpre, vpre)


def _decode_kernel(pt_ref, ql_ref, qp_ref, cn_ref, kn_ref, ckv_hbm, kpe_hbm, o_ref,
                   cbuf, kbuf, ncbuf, nkbuf, sem, m_s, l_s, acc_s, *, layer, G, NC, T):
    b = pl.program_id(0)
    c = pl.program_id(1)
    n = b * NC + c
    total = pl.num_programs(0) * NC
    slot = lax.rem(n, 2)

    def copies(bb, cc, sl):
        out = []
        for g in range(G):
            page = pt_ref[bb, cc * G + g]
            out.append(pltpu.make_async_copy(ckv_hbm.at[layer, page], cbuf.at[sl, g], sem.at[0, sl]))
            out.append(pltpu.make_async_copy(kpe_hbm.at[layer, page], kbuf.at[sl, g], sem.at[1, sl]))
        return out

    @pl.when(n == 0)
    def _():
        ncbuf[...] = jnp.zeros_like(ncbuf)
        nkbuf[...] = jnp.zeros_like(nkbuf)
        for cp in copies(b, c, slot):
            cp.start()

    @pl.when(n + 1 < total)
    def _():
        n1 = n + 1
        for cp in copies(n1 // NC, lax.rem(n1, NC), 1 - slot):
            cp.start()

    pltpu.make_async_copy(ckv_hbm.at[layer, pl.ds(0, G)], cbuf.at[slot], sem.at[0, slot]).wait()
    pltpu.make_async_copy(kpe_hbm.at[layer, pl.ds(0, G)], kbuf.at[slot], sem.at[1, slot]).wait()

    ql = ql_ref[...].astype(F32)
    qp = qp_ref[...].astype(F32)

    @pl.when(c == 0)
    def _():
        m_s[...] = jnp.full_like(m_s, -jnp.inf)
        l_s[...] = jnp.zeros_like(l_s)
        acc_s[...] = jnp.zeros_like(acc_s)

    def partial_softmax(kvs, kpts, mask_fn=None):
        ss = [_dot_nt(ql, kv) + _dot(qp, kpt) for kv, kpt in zip(kvs, kpts)]
        if mask_fn is not None:
            ss = [mask_fn(s) for s in ss]
        ms = [jnp.max(s, axis=-1, keepdims=True) for s in ss]
        ps = [jnp.exp(s - m) for s, m in zip(ss, ms)]
        ls = [jnp.sum(p, axis=-1, keepdims=True) for p in ps]
        accs = [_dot(p, kv) for p, kv in zip(ps, kvs)]
        return list(zip(ms, ls, accs))

    def merge(parts):
        m_old = m_s[...]
        m_new = m_old
        for m, _, _ in parts:
            m_new = jnp.maximum(m_new, m)
        w_old = jnp.exp(m_old - m_new)
        l = w_old * l_s[...]
        acc = w_old * acc_s[...]
        for m, li, ai in parts:
            wi = jnp.exp(m - m_new)
            l = l + wi * li
            acc = acc + wi * ai
        return m_new, l, acc

    pp = G // DEC_SPLIT
    kvs = [cbuf[slot, sc * pp:(sc + 1) * pp].reshape(pp * PAGE_SIZE, KV_LORA) for sc in range(DEC_SPLIT)]
    kpts = [jnp.concatenate([kbuf[slot, sc * pp + g] for g in range(pp)], axis=1)
            for sc in range(DEC_SPLIT)]
    m_s[...], l_s[...], acc_s[...] = merge(partial_softmax(kvs, kpts))

    @pl.when(c == NC - 1)
    def _():
        ncbuf[0:T, :] = cn_ref[...]
        nkbuf[:, 0:T] = kn_ref[...]

        def mask(s):
            tq = lax.rem(lax.broadcasted_iota(jnp.int32, s.shape, 0), T)
            col = lax.broadcasted_iota(jnp.int32, s.shape, 1)
            return jnp.where(col <= tq, s, -jnp.inf)

        m, l, acc = merge(partial_softmax([ncbuf[...]], [nkbuf[...]], mask))
        o_ref[...] = acc / l


def _decode_attn(page_table, qlat, qpe, ckv_new, kpe_new_t, cache_ckv, cache_kpe_t, *, layer):
    DB, rows, _ = qlat.shape
    T = ckv_new.shape[1]
    n_pages = page_table.shape[1]
    G = min(DEC_PAGES, n_pages)
    assert n_pages % G == 0 and G % DEC_SPLIT == 0
    NC = n_pages // G
    grid_spec = pltpu.PrefetchScalarGridSpec(
        num_scalar_prefetch=1,
        grid=(DB, NC),
        in_specs=[
            pl.BlockSpec((None, rows, KV_LORA), lambda b, c, pt: (b, 0, 0)),
            pl.BlockSpec((None, rows, QK_ROPE), lambda b, c, pt: (b, 0, 0)),
            pl.BlockSpec((None, T, KV_LORA), lambda b, c, pt: (b, 0, 0)),
            pl.BlockSpec((None, QK_ROPE, T), lambda b, c, pt: (b, 0, 0)),
            pl.BlockSpec(memory_space=pl.ANY),
            pl.BlockSpec(memory_space=pl.ANY),
        ],
        out_specs=pl.BlockSpec((None, rows, KV_LORA), lambda b, c, pt: (b, 0, 0)),
        scratch_shapes=[
            pltpu.VMEM((2, G, PAGE_SIZE, KV_LORA), F32),
            pltpu.VMEM((2, G, QK_ROPE, PAGE_SIZE), F32),
            pltpu.VMEM((PAGE_SIZE, KV_LORA), F32),
            pltpu.VMEM((QK_ROPE, PAGE_SIZE), F32),
            pltpu.SemaphoreType.DMA((2, 2)),
            pltpu.VMEM((rows, 1), F32), pltpu.VMEM((rows, 1), F32), pltpu.VMEM((rows, KV_LORA), F32),
        ],
    )
    return pl.pallas_call(
        functools.partial(_decode_kernel, layer=layer, G=G, NC=NC, T=T),
        grid_spec=grid_spec,
        out_shape=jax.ShapeDtypeStruct((DB, rows, KV_LORA), F32),
        name="decode_attn",
        compiler_params=pltpu.CompilerParams(
            dimension_semantics=("arbitrary", "arbitrary"), vmem_limit_bytes=VMEM_LIMIT),
    )(page_table, qlat, qpe, ckv_new, kpe_new_t, cache_ckv, cache_kpe_t)


def _ffn_tail(x1, nf_ref, wg_ref, wu_ref, wd_ref, fn_ref, o_ref, acc_s, final):
    xn = _rms(x1, nf_ref[...]).astype(BF16)
    for ci in range(D_FF // FF_CHUNK):
        sl = slice(ci * FF_CHUNK, (ci + 1) * FF_CHUNK)
        hg = _dot(xn, wg_ref[:, sl])
        hu = _dot(xn, wu_ref[:, sl])
        act = ((hg * jax.nn.sigmoid(hg)) * hu).astype(BF16)
        part = _dot(act, wd_ref[sl, :])
        if ci == 0:
            acc_s[...] = part
        else:
            acc_s[...] += part
    x2 = x1 + acc_s[...]
    if final:
        x2 = _rms(x2, fn_ref[...])
    o_ref[...] = x2


def _out_kernel(x_ref, a_ref, r_ref, wo_ref, nf_ref, wg_ref, wu_ref, wd_ref, fn_ref, o_ref, acc_s,
                *, final):
    na = a_ref.shape[1]
    y = _dot(a_ref[...], wo_ref[0:na, :]) + _dot(r_ref[...], wo_ref[na:, :])
    _ffn_tail(x_ref[...] + y, nf_ref, wg_ref, wu_ref, wd_ref, fn_ref, o_ref, acc_s, final)


def _out_latent_kernel(x_ref, ol_ref, wuv_ref, r_ref, wo_ref, nf_ref, wg_ref, wu_ref, wd_ref, fn_ref,
                       o_ref, acc_s, *, final):
    na = N_HEADS * V_DIM
    y = _dot(r_ref[...], wo_ref[na:, :])
    for h in range(N_HEADS):
        ah = _dot(ol_ref[h], wuv_ref[h]).astype(BF16)
        y = y + _dot(ah, wo_ref[h * V_DIM:(h + 1) * V_DIM, :])
    _ffn_tail(x_ref[...] + y, nf_ref, wg_ref, wu_ref, wd_ref, fn_ref, o_ref, acc_s, final)


def _out_proj(x, attn, rnn, w, final_norm, *, tm, final, latent=False):
    n = x.shape[0]
    assert n % tm == 0
    row = lambda c: pl.BlockSpec((tm, c), lambda i: (i, 0))
    wspecs = [_const_spec((D_MODEL, D_MODEL)), _const_spec((1, D_MODEL)),
              _const_spec((D_MODEL, D_FF)), _const_spec((D_MODEL, D_FF)), _const_spec((D_FF, D_MODEL)),
              _const_spec((1, D_MODEL))]
    wargs = (w["w_out"], w["norm_ffn"], w["w_gate"], w["w_up"], w["w_down"], final_norm)
    if latent:
        kern = functools.partial(_out_latent_kernel, final=final)
        in_specs = [row(D_MODEL), pl.BlockSpec((N_HEADS, tm, KV_LORA), lambda i: (0, i, 0)),
                    _const_spec((N_HEADS, KV_LORA, V_DIM)), row(D_RNN)] + wspecs
        args = (x, attn, w["w_uv3"], rnn) + wargs
    else:
        kern = functools.partial(_out_kernel, final=final)
        in_specs = [row(D_MODEL), row(N_HEADS * V_DIM), row(D_RNN)] + wspecs
        args = (x, attn, rnn) + wargs
    return pl.pallas_call(
        kern, grid=(n // tm,), in_specs=in_specs, out_specs=row(D_MODEL),
        out_shape=jax.ShapeDtypeStruct((n, D_MODEL), F32),
        scratch_shapes=[pltpu.VMEM((tm, D_MODEL), F32)],
        name="out_latent" if latent else "out_proj",
        compiler_params=pltpu.CompilerParams(
            dimension_semantics=("arbitrary",), vmem_limit_bytes=VMEM_LIMIT),
    )(*args)


def _rot_half_cols(w):
    half = QK_ROPE // 2
    return jnp.concatenate([-w[..., half:], w[..., :half]], axis=-1)


def _prep_layer(l, p):
    w_in = p["w_in"][l]
    d = w_in.shape[0]
    o_kpe = Q_LORA + KV_LORA
    kpe = w_in[:, o_kpe:o_kpe + QK_ROPE]
    z = lambda c: jnp.zeros((d, c), F32)
    w_in_r = jnp.concatenate(
        [w_in[:, :o_kpe], w_in[:, o_kpe + QK_ROPE:],
         z(QK_NOPE), kpe, z(LANES - QK_NOPE - QK_ROPE),
         z(QK_NOPE), _rot_half_cols(kpe), z(LANES - QK_NOPE - QK_ROPE)], axis=1)

    w_uq = p["w_uq"][l].reshape(Q_LORA, N_HEADS, QK_NOPE + QK_ROPE)
    nope, pe = w_uq[..., :QK_NOPE], w_uq[..., QK_NOPE:]
    zq = lambda c: jnp.zeros((Q_LORA, N_HEADS, c), F32)
    qa = jnp.concatenate([nope, pe, zq(LANES - QK_NOPE - QK_ROPE)], axis=-1)
    qb = jnp.concatenate([zq(QK_NOPE), _rot_half_cols(pe), zq(LANES - QK_NOPE - QK_ROPE)], axis=-1)
    w_uq_r = jnp.concatenate([qa.reshape(Q_LORA, -1), qb.reshape(Q_LORA, -1)], axis=1)

    w_uk = p["w_uk"][l]
    w_uv = p["w_uv"][l]
    w_k = jnp.concatenate([w_uk, jnp.zeros((KV_LORA, N_HEADS, LANES - QK_NOPE), F32)], axis=-1)
    w_kv = jnp.concatenate([w_k.reshape(KV_LORA, -1), w_uv.reshape(KV_LORA, -1)], axis=1)
    w_ukt = jnp.concatenate(
        [jnp.transpose(w_uk, (1, 2, 0)), jnp.zeros((N_HEADS, LANES - QK_NOPE, KV_LORA), F32)], axis=1)

    eye = jnp.eye(N_RNN_BLOCKS, dtype=F32)
    bd = lambda wb: jnp.einsum("nde,nm->ndme", wb, eye).reshape(D_RNN, D_RNN)
    w_gates = jnp.concatenate([bd(p["w_ra"][l]), bd(p["w_ri"][l])], axis=1)
    row = lambda v: v.reshape(1, -1).astype(F32)
    return {
        "norm_mix": row(p["norm_mix"][l]), "w_in": w_in_r.astype(BF16),
        "q_norm": row(p["q_norm"][l]), "w_uq": w_uq_r.astype(BF16),
        "kv_norm": row(p["kv_norm"][l]), "w_kv": w_kv.astype(BF16), "w_ukt": w_ukt.astype(BF16),
        "w_uv3": jnp.transpose(w_uv, (1, 0, 2)).astype(BF16),
        "conv_w": p["conv_w"][l].astype(F32), "conv_b": row(p["conv_b"][l]),
        "w_gates": w_gates.astype(BF16),
        "b_gates": jnp.concatenate([p["b_ra"][l], p["b_ri"][l]]).reshape(1, -1).astype(F32),
        "lru_a": row(p["lru_a"][l]),
        "w_out": p["w_out"][l].astype(BF16), "norm_ffn": row(p["norm_ffn"][l]),
        "w_gate": p["w_gate"][l].astype(BF16), "w_up": p["w_up"][l].astype(BF16),
        "w_down": p["w_down"][l].astype(BF16),
    }


def _rope_table(pos):
    half = QK_ROPE // 2
    inv = ROPE_THETA ** (-jnp.arange(half, dtype=F32) / half)
    ang = pos[:, None] * inv[None, :]
    cos = jnp.tile(jnp.cos(ang), (1, 2))
    sin = jnp.tile(jnp.sin(ang), (1, 2))
    n = pos.shape[0]
    one = jnp.ones((n, QK_NOPE), F32)
    z_lo = jnp.zeros((n, QK_NOPE), F32)
    z_hi = jnp.zeros((n, LANES - QK_NOPE - QK_ROPE), F32)
    return jnp.concatenate(
        [one * ATTN_SCALE, cos * ATTN_SCALE, z_hi, z_lo, sin * ATTN_SCALE, z_hi,
         z_lo, cos, z_hi, z_lo, sin, z_hi], axis=1)


def _pick_tile(n, pref):
    t = min(pref, n)
    while n % t:
        t //= 2
    return t


def kernel(x_prompt, x_sample, cache_ckv, cache_kpe, state_h, state_conv, page_table, meta_tokens,
           norm_mix, w_in, q_norm, w_uq, kv_norm, w_uk, w_uv, conv_w, conv_b, w_ra, b_ra, w_ri, b_ri,
           lru_a, w_out, norm_ffn, w_gate, w_up, w_down, final_norm):
    params = dict(norm_mix=norm_mix, w_in=w_in, q_norm=q_norm, w_uq=w_uq, kv_norm=kv_norm, w_uk=w_uk,
                  w_uv=w_uv, conv_w=conv_w, conv_b=conv_b, w_ra=w_ra, b_ra=b_ra, w_ri=w_ri, b_ri=b_ri,
                  lru_a=lru_a, w_out=w_out, norm_ffn=norm_ffn, w_gate=w_gate, w_up=w_up, w_down=w_down)
    depth = w_in.shape[0]
    B, S, _ = x_prompt.shape
    DB, T, _ = x_sample.shape
    assert T >= CONV_W - 1
    past_len = page_table.shape[1] * PAGE_SIZE
    fnorm = final_norm.reshape(1, -1).astype(F32)

    tab_meta = _rope_table(jnp.arange(META_ROWS, dtype=F32) - META_PAD)
    tab_main = _rope_table(N_META + jnp.arange(S, dtype=F32))
    tab_samp = _rope_table(jnp.repeat(past_len + jnp.arange(T, dtype=F32), DB))

    xm = jnp.concatenate([jnp.zeros((META_PAD, D_MODEL), F32), meta_tokens.astype(F32)], axis=0)[None]
    xp = x_prompt
    xs = jnp.transpose(x_sample, (1, 0, 2)).reshape(T * DB, D_MODEL)
    zero_h = jnp.zeros((1, 1, D_RNN), F32)
    zero_c = jnp.zeros((1, SUBLANES, D_RNN), F32)
    tm_in = _pick_tile(S, 512)
    tm_out = _pick_tile(B * S, 512)
    tm_s = _pick_tile(T * DB, 512)

    cache_kpe_t = jnp.transpose(cache_kpe, (0, 1, 3, 2))
    outs = {k: [] for k in ("ckv_p", "kpe_p", "h_p", "cv_p", "ckv_s", "kpe_s", "h_s", "cv_s")}
    rope_lanes = slice(QK_NOPE, QK_NOPE + QK_ROPE)
    for l in range(depth):
        w = _prep_layer(l, params)
        last = l == depth - 1

        ckv_m, kpe_m, q_m, k_m, v_m, rnn_m, h_m, c_m = _in_prompt(
            xm, tab_meta, zero_h, zero_c, w, tm=META_ROWS, n_pad=META_PAD)
        attn_m = _attn_meta(q_m[0], k_m[0], v_m[0], n_pad=META_PAD)
        xm = _out_proj(xm[0], attn_m, rnn_m[0], w, fnorm, tm=META_ROWS, final=False)[None]

        ckv, kpe, q, k, v, rnn, h_f, c_f = _in_prompt(xp, tab_main, h_m, c_m, w, tm=tm_in, n_pad=0)
        attn = _attn_main(q, k, v, k_m[0], v_m[0], n_pad=META_PAD)
        xp = _out_proj(xp.reshape(B * S, D_MODEL), attn.reshape(B * S, -1), rnn.reshape(B * S, -1),
                       w, fnorm, tm=tm_out, final=last).reshape(B, S, D_MODEL)
        bc = lambda a: jnp.broadcast_to(a[None], (B,) + a.shape)
        outs["ckv_p"].append(jnp.concatenate([bc(ckv_m[0, META_PAD:]), ckv], axis=1))
        outs["kpe_p"].append(jnp.concatenate(
            [bc(kpe_m[0, META_PAD:, rope_lanes]), kpe[..., rope_lanes]], axis=1))
        outs["h_p"].append(h_f[:, 0])
        outs["cv_p"].append(c_f[:, SUBLANES - (CONV_W - 1):])

        cp = jnp.transpose(state_conv[l], (1, 0, 2))
        ckv_s, kpe_s, qlat, qrot, rnn_s, h_s, cv_s = _in_sample(
            xs, tab_samp, state_h[l], cp, w, nb=DB, nt=T)
        ckv_new = jnp.transpose(ckv_s.reshape(T, DB, KV_LORA), (1, 0, 2))
        kpe_new = jnp.transpose(kpe_s[:, rope_lanes].reshape(T, DB, QK_ROPE), (1, 0, 2))
        ql = jnp.transpose(qlat.reshape(N_HEADS, T, DB, KV_LORA), (2, 0, 1, 3)).reshape(DB, N_HEADS * T, KV_LORA)
        qp = qrot.reshape(T, DB, N_HEADS, LANES)[..., rope_lanes]
        qp = jnp.transpose(qp, (1, 2, 0, 3)).reshape(DB, N_HEADS * T, QK_ROPE)
        o_lat = _decode_attn(page_table, ql, qp, ckv_new, jnp.transpose(kpe_new, (0, 2, 1)),
                             cache_ckv, cache_kpe_t, layer=l)
        ol = jnp.transpose(o_lat.reshape(DB, N_HEADS, T, KV_LORA), (1, 2, 0, 3))
        ol = ol.reshape(N_HEADS, T * DB, KV_LORA).astype(BF16)
        xs = _out_proj(xs, ol, rnn_s, w, fnorm, tm=tm_s, final=last, latent=True)
        outs["ckv_s"].append(ckv_new)
        outs["kpe_s"].append(kpe_new)
        outs["h_s"].append(h_s)
        outs["cv_s"].append(jnp.transpose(cv_s, (1, 0, 2)))

    y_sample = jnp.transpose(xs.reshape(T, DB, D_MODEL), (1, 0, 2))
    st = lambda k: jnp.stack(outs[k])
    return (xp, y_sample, st("ckv_p"), st("kpe_p"), st("h_p"), st("cv_p"),
            st("ckv_s"), st("kpe_s"), st("h_s"), st("cv_s"))
```
